```python
import math
import jax, jax.numpy as jnp
from jax import lax
import numpy as np

D_MODEL = 1024
BATCH = 8
SEQ = 8192
DEPTH = 2

N_MIXERS = 2
D_FF = 2816
FFN_RES = 0.5
POOL_WINDOWS = (2, 4, 8, 16)
POOL_GROUPS = len(POOL_WINDOWS)
POOL_GW = D_MODEL // POOL_GROUPS
N_HEADS = 8
HEAD_DIM = D_MODEL // (2 * N_HEADS)
V_DIM = 2 * HEAD_DIM
QK_W = N_HEADS * HEAD_DIM
QKV_W = 4 * QK_W + N_HEADS * V_DIM
ATTN_SCALE = HEAD_DIM ** -0.5
Q_BLOCK = 128
LN_EPS = 1e-5
RMS_EPS = 1e-5
DEEPNORM_ALPHA = (2.0 * DEPTH) ** 0.25
DEEPNORM_BETA = (8.0 * DEPTH) ** -0.25

kernel_name = "hybrid_pool_diffattn_macaron_deepnorm"


def layer_norm(x, g, b):
    xf = x.astype(jnp.float32)
    mu = jnp.mean(xf, axis=-1, keepdims=True)
    var = jnp.mean(jnp.square(xf - mu), axis=-1, keepdims=True)
    y = (xf - mu) * lax.rsqrt(var + LN_EPS) * g.astype(jnp.float32) + b.astype(jnp.float32)
    return y.astype(x.dtype)


def deepnorm(x, y, g, b):
    return layer_norm(DEEPNORM_ALPHA * x + y, g, b)


def swiglu(x, w_in, w_out):
    gate, up = jnp.split(x @ w_in, 2, axis=-1)
    return (jax.nn.silu(gate) * up) @ w_out


def pool_mixer(x, w_pool, scale):
    B, S, D = x.shape
    xf = x.astype(jnp.float32)
    c = jnp.cumsum(xf, axis=1)
    pos = jnp.arange(S)
    outs = []
    for g, w in enumerate(POOL_WINDOWS):
        sl = slice(g * POOL_GW, (g + 1) * POOL_GW)
        cg = c[..., sl]
        shifted = jnp.pad(cg[:, :S - w], ((0, 0), (w, 0), (0, 0)))
        count = jnp.minimum(pos + 1, w).astype(jnp.float32)[None, :, None]
        d = ((cg - shifted) / count - xf[..., sl]).astype(x.dtype)
        outs.append(d @ w_pool[g])
    return jnp.concatenate(outs, axis=-1) * scale


def diff_attention(x, w_qkv, lam_q1, lam_k1, lam_q2, lam_k2, subln_g, w_o, lambda_init):
    B, S, _ = x.shape
    qkv = x @ w_qkv
    q1, q2, k1, k2, v = jnp.split(qkv, [QK_W, 2 * QK_W, 3 * QK_W, 4 * QK_W], axis=-1)

    def heads(t, d):
        return t.reshape(B, S, N_HEADS, d).transpose(0, 2, 1, 3)

    q1, q2, k1, k2 = (heads(t, HEAD_DIM) for t in (q1, q2, k1, k2))
    vf = heads(v, V_DIM).astype(jnp.float32)

    lam = (jnp.exp(jnp.sum(lam_q1.astype(jnp.float32) * lam_k1.astype(jnp.float32)))
           - jnp.exp(jnp.sum(lam_q2.astype(jnp.float32) * lam_k2.astype(jnp.float32)))
           + lambda_init)

    slopes = jnp.exp2(-8.0 * jnp.arange(1, N_HEADS + 1, dtype=jnp.float32) / N_HEADS)
    n_blocks = S // Q_BLOCK
    k_pos = jnp.arange(S)

    def to_blocks(t):
        return t.reshape(B, N_HEADS, n_blocks, Q_BLOCK, HEAD_DIM).transpose(2, 0, 1, 3, 4)

    def one_block(args):
        q1b, q2b, blk = args
        q_pos = blk * Q_BLOCK + jnp.arange(Q_BLOCK)
        dist = (q_pos[:, None] - k_pos[None, :]).astype(jnp.float32)
        causal = dist >= 0
        bias = -slopes[:, None, None] * dist

        def probs(qb, k):
            s = jnp.einsum('bhqd,bhkd->bhqk', qb, k).astype(jnp.float32) * ATTN_SCALE + bias
            return jax.nn.softmax(jnp.where(causal, s, -jnp.inf), axis=-1)

        a = probs(q1b, k1) - lam * probs(q2b, k2)
        return jnp.einsum('bhqk,bhke->bhqe', a, vf)

    o = lax.map(one_block, (to_blocks(q1), to_blocks(q2), jnp.arange(n_blocks)))
    o = o.transpose(1, 0, 3, 2, 4).reshape(B, S, N_HEADS, V_DIM)
    o = o * lax.rsqrt(jnp.mean(jnp.square(o), axis=-1, keepdims=True) + RMS_EPS)
    o = o * subln_g.astype(jnp.float32) * (1.0 - lambda_init)
    return o.reshape(B, S, N_HEADS * V_DIM).astype(x.dtype) @ w_o


def setup_inputs(seed: int = 0) -> dict:
    key = jax.random.key(seed)
    ks = iter(jax.random.split(key, 64))
    f32 = jnp.float32

    def nrm(shape, scale):
        return jax.random.normal(next(ks), shape, f32) * scale

    def gain(n):
        return 1.0 + 0.02 * jax.random.normal(next(ks), (n,), f32)

    def bias(n):
        return 0.02 * jax.random.normal(next(ks), (n,), f32)

    def ffn():
        return (nrm((D_MODEL, 2 * D_FF), D_MODEL ** -0.5),
                nrm((D_FF, D_MODEL), D_FF ** -0.5 * DEEPNORM_BETA))

    inp = {"x": jax.random.normal(next(ks), (BATCH, SEQ, D_MODEL), f32)}
    inp["l0_ffn1_w_in"], inp["l0_ffn1_w_out"] = ffn()
    inp["l0_ln1_g"], inp["l0_ln1_b"] = gain(D_MODEL), bias(D_MODEL)
    inp["l0_pool_w"] = nrm((POOL_GROUPS, POOL_GW, POOL_GW), POOL_GW ** -0.5 * DEEPNORM_BETA)
    inp["l0_pool_scale"] = gain(D_MODEL)
    inp["l0_ln2_g"], inp["l0_ln2_b"] = gain(D_MODEL), bias(D_MODEL)
    inp["l0_ffn2_w_in"], inp["l0_ffn2_w_out"] = ffn()
    inp["l0_ln3_g"], inp["l0_ln3_b"] = gain(D_MODEL), bias(D_MODEL)
    inp["l1_ffn1_w_in"], inp["l1_ffn1_w_out"] = ffn()
    inp["l1_ln1_g"], inp["l1_ln1_b"] = gain(D_MODEL), bias(D_MODEL)
    inp["l1_w_qkv"] = nrm((D_MODEL, QKV_W), D_MODEL ** -0.5)
    inp["l1_lam_q1"] = nrm((HEAD_DIM,), 0.1)
    inp["l1_lam_k1"] = nrm((HEAD_DIM,), 0.1)
    inp["l1_lam_q2"] = nrm((HEAD_DIM,), 0.1)
    inp["l1_lam_k2"] = nrm((HEAD_DIM,), 0.1)
    inp["l1_subln_g"] = gain(V_DIM)
    inp["l1_w_o"] = nrm((N_HEADS * V_DIM, D_MODEL), (N_HEADS * V_DIM) ** -0.5 * DEEPNORM_BETA)
    inp["l1_ln2_g"], inp["l1_ln2_b"] = gain(D_MODEL), bias(D_MODEL)
    inp["l1_ffn2_w_in"], inp["l1_ffn2_w_out"] = ffn()
    inp["l1_ln3_g"], inp["l1_ln3_b"] = gain(D_MODEL), bias(D_MODEL)
    return inp


def reference(x,
              l0_ffn1_w_in, l0_ffn1_w_out, l0_ln1_g, l0_ln1_b,
              l0_pool_w, l0_pool_scale, l0_ln2_g, l0_ln2_b,
              l0_ffn2_w_in, l0_ffn2_w_out, l0_ln3_g, l0_ln3_b,
              l1_ffn1_w_in, l1_ffn1_w_out, l1_ln1_g, l1_ln1_b,
              l1_w_qkv, l1_lam_q1, l1_lam_k1, l1_lam_q2, l1_lam_k2, l1_subln_g, l1_w_o,
              l1_ln2_g, l1_ln2_b,
              l1_ffn2_w_in, l1_ffn2_w_out, l1_ln3_g, l1_ln3_b):
    layers = [
        dict(ffn1=(l0_ffn1_w_in, l0_ffn1_w_out), ln1=(l0_ln1_g, l0_ln1_b),
             mix=(l0_pool_w, l0_pool_scale), ln2=(l0_ln2_g, l0_ln2_b),
             ffn2=(l0_ffn2_w_in, l0_ffn2_w_out), ln3=(l0_ln3_g, l0_ln3_b)),
        dict(ffn1=(l1_ffn1_w_in, l1_ffn1_w_out), ln1=(l1_ln1_g, l1_ln1_b),
             mix=(l1_w_qkv, l1_lam_q1, l1_lam_k1, l1_lam_q2, l1_lam_k2, l1_subln_g, l1_w_o),
             ln2=(l1_ln2_g, l1_ln2_b),
             ffn2=(l1_ffn2_w_in, l1_ffn2_w_out), ln3=(l1_ln3_g, l1_ln3_b)),
    ]
    for i in range(DEPTH):
        p = layers[i]
        x = deepnorm(x, FFN_RES * swiglu(x, *p["ffn1"]), *p["ln1"])
        if i % N_MIXERS == 0:
            y = pool_mixer(x, *p["mix"])
        else:
            lambda_init = 0.8 - 0.6 * math.exp(-0.3 * i)
            y = diff_attention(x, *p["mix"], lambda_init)
        x = deepnorm(x, y, *p["ln2"])
        x = deepnorm(x, FFN_RES * swiglu(x, *p["ffn2"]), *p["ln3"])
    return x
```

```python
import functools
import math

import jax
import jax.numpy as jnp
from jax import lax
from jax.experimental import pallas as pl
from jax.experimental.pallas import tpu as pltpu

F32 = jnp.float32
BF16 = jnp.bfloat16

DEPTH = 2
FFN_RES = 0.5
POOL_WINDOWS = (2, 4, 8, 16)
POOL_HALO = 16
N_HEADS = 8
LN_EPS = 1e-5
RMS_EPS = 1e-5
DEEPNORM_ALPHA = (2.0 * DEPTH) ** 0.25

V7X_LANES = 128
V7X_VMEM_BYTES = 64 * 1024 * 1024
V7X_VMEM_RESERVE_BYTES = 8 * 1024 * 1024

POS_SPLIT_BITS = 6
MAX_SEQ = 1 << (POS_SPLIT_BITS + 8)


def _vmem_limit(*byte_counts):
    need = int(sum(byte_counts))
    return min(need + V7X_VMEM_RESERVE_BYTES, V7X_VMEM_BYTES - V7X_VMEM_RESERVE_BYTES)


def _nbytes(shape, dtype):
    return math.prod(shape) * jnp.dtype(dtype).itemsize


def _layer_norm(y, g, b):
    mu = jnp.mean(y, axis=-1, keepdims=True)
    yc = y - mu
    var = jnp.mean(yc * yc, axis=-1, keepdims=True)
    return yc * lax.rsqrt(var + LN_EPS) * g + b


def _const2(i):
    return (0, 0)


def _ffn_ln_body(x_ref, win_ref, wout_ref, g_ref, b_ref, o_ref, acc_ref, *, d_ff, chunk):
    x = x_ref[...]
    xb = x.astype(BF16)
    for c in range(d_ff // chunk):
        lo = c * chunk
        gate = jnp.dot(xb, win_ref[:, lo:lo + chunk], preferred_element_type=F32)
        up = jnp.dot(xb, win_ref[:, d_ff + lo:d_ff + lo + chunk], preferred_element_type=F32)
        act = (gate * jax.nn.sigmoid(gate) * up).astype(BF16)
        part = jnp.dot(act, wout_ref[lo:lo + chunk, :], preferred_element_type=F32)
        if c == 0:
            acc_ref[...] = part
        else:
            acc_ref[...] += part
    y = DEEPNORM_ALPHA * x + FFN_RES * acc_ref[...]
    o_ref[...] = _layer_norm(y, g_ref[...], b_ref[...])


def _ffn_ln(x, w_in, w_out, g, b, *, tm=512, chunk=256):
    n, d = x.shape
    d_ff = w_out.shape[0]
    assert n % tm == 0 and d_ff % chunk == 0 and w_in.shape == (d, 2 * d_ff)
    tile = _nbytes((tm, d), F32)
    vmem = _vmem_limit(_nbytes(w_in.shape, BF16), _nbytes(w_out.shape, BF16),
                       4 * tile,
                       tile,
                       3 * tile)
    return pl.pallas_call(
        functools.partial(_ffn_ln_body, d_ff=d_ff, chunk=chunk),
        out_shape=jax.ShapeDtypeStruct((n, d), F32),
        grid=(n // tm,),
        in_specs=[
            pl.BlockSpec((tm, d), lambda i: (i, 0)),
            pl.BlockSpec((d, 2 * d_ff), _const2, pipeline_mode=pl.Buffered(1)),
            pl.BlockSpec((d_ff, d), _const2, pipeline_mode=pl.Buffered(1)),
            pl.BlockSpec((1, d), _const2),
            pl.BlockSpec((1, d), _const2),
        ],
        out_specs=pl.BlockSpec((tm, d), lambda i: (i, 0)),
        scratch_shapes=[pltpu.VMEM((tm, d), F32)],
        compiler_params=pltpu.CompilerParams(dimension_semantics=("arbitrary",),
                                             vmem_limit_bytes=vmem),
        name="ffn_ln",
    )(x, w_in, w_out, g, b)


def _pool_ln_body(x_ref, halo_ref, w_ref, scale_ref, g_ref, b_ref, o_ref, ext_ref, *, seq_tiles):
    tm, d = x_ref.shape
    t = pl.program_id(0) % seq_tiles
    x = x_ref[...]
    ext_ref[0:POOL_HALO, :] = jnp.where(t == 0, 0.0, halo_ref[...])
    ext_ref[POOL_HALO:, :] = x
    pos = t * tm + lax.broadcasted_iota(jnp.int32, (tm, 1), 0)
    gw = d // len(POOL_WINDOWS)
    outs = []
    for gi, w in enumerate(POOL_WINDOWS):
        cols = slice(gi * gw, (gi + 1) * gw)
        wsum = x[:, cols]
        for k in range(1, w):
            wsum = wsum + ext_ref[POOL_HALO - k:POOL_HALO - k + tm, cols]
        count = jnp.minimum(pos + 1, w).astype(F32)
        dg = (wsum / count - x[:, cols]).astype(BF16)
        outs.append(jnp.dot(dg, w_ref[gi], preferred_element_type=F32))
    y = jnp.concatenate(outs, axis=-1) * scale_ref[...]
    o_ref[...] = _layer_norm(DEEPNORM_ALPHA * x + y, g_ref[...], b_ref[...])


def _pool_ln(x, w_pool, scale, g, b, *, seq, tm=512):
    n, d = x.shape
    assert seq % tm == 0 and tm % POOL_HALO == 0 and max(POOL_WINDOWS) <= POOL_HALO
    halo_blocks_per_tile = tm // POOL_HALO
    tile = _nbytes((tm, d), F32)
    vmem = _vmem_limit(2 * _nbytes(w_pool.shape, BF16), 4 * tile, tile, 4 * tile)
    return pl.pallas_call(
        functools.partial(_pool_ln_body, seq_tiles=seq // tm),
        out_shape=jax.ShapeDtypeStruct((n, d), F32),
        grid=(n // tm,),
        in_specs=[
            pl.BlockSpec((tm, d), lambda i: (i, 0)),
            pl.BlockSpec((POOL_HALO, d),
                         lambda i: (jnp.maximum(i * halo_blocks_per_tile - 1, 0), 0)),
            pl.BlockSpec(w_pool.shape, lambda i: (0, 0, 0)),
            pl.BlockSpec((1, d), _const2),
            pl.BlockSpec((1, d), _const2),
            pl.BlockSpec((1, d), _const2),
        ],
        out_specs=pl.BlockSpec((tm, d), lambda i: (i, 0)),
        scratch_shapes=[pltpu.VMEM((tm + POOL_HALO, d), F32)],
        compiler_params=pltpu.CompilerParams(dimension_semantics=("arbitrary",),
                                             vmem_limit_bytes=vmem),
        name="pool_ln",
    )(x, x, w_pool, scale, g, b)


_NT_DIMS = (((1,), (1,)), ((), ()))


def _qkv_body(x_ref, wqT_ref, wk_ref, wvT_ref, q1T_ref, q2T_ref, k1_ref, k2_ref, vT_ref,
              *, head_dim, scale):
    t = pl.program_id(1)
    ts = x_ref.shape[0]
    width = 2 * head_dim
    xb = x_ref[...].astype(BF16)
    qT = lax.dot_general(wqT_ref[...], xb, _NT_DIMS, preferred_element_type=F32) * scale
    vT = lax.dot_general(wvT_ref[...], xb, _NT_DIMS, preferred_element_type=F32)
    kk = jnp.dot(xb, wk_ref[...], preferred_element_type=F32)

    r = lax.broadcasted_iota(jnp.int32, qT.shape, 0) & (width - 1)
    q1T_ref[0, 0] = jnp.where(r < head_dim, qT,
                              jnp.where(r < head_dim + 2, 1.0, 0.0)).astype(BF16)
    q2T_ref[0, 0] = jnp.where(r >= head_dim, qT, jnp.where(r < 2, 1.0, 0.0)).astype(BF16)
    vT_ref[0, 0] = vT.astype(BF16)

    col = lax.broadcasted_iota(jnp.int32, (1, kk.shape[1]), 1)
    c = col & (width - 1)
    slope = jnp.exp2(-((col // width) + 1).astype(F32))
    pos = t * ts + lax.broadcasted_iota(jnp.int32, (ts, 1), 0)
    hi = ((pos >> POS_SPLIT_BITS) << POS_SPLIT_BITS).astype(F32)
    lo = (pos & ((1 << POS_SPLIT_BITS) - 1)).astype(F32)
    b_hi = slope * hi
    b_lo = slope * lo
    k1_ref[0] = jnp.where(c < head_dim, kk,
                          jnp.where(c == head_dim, b_hi,
                                    jnp.where(c == head_dim + 1, b_lo, 0.0))).astype(BF16)
    k2_ref[0] = jnp.where(c >= head_dim, kk,
                          jnp.where(c == 0, b_hi, jnp.where(c == 1, b_lo, 0.0))).astype(BF16)


def _qkv_proj(x, wqT, wk, wvT, *, batch, seq, ts, head_dim, scale):
    n, d = x.shape
    hw = wk.shape[1]
    st = seq // ts
    assert seq % ts == 0 and seq <= MAX_SEQ
    tile = _nbytes((ts, hw), F32)
    vmem = _vmem_limit(2 * 3 * _nbytes(wk.shape, BF16), 2 * _nbytes((ts, d), F32),
                       2 * 5 * _nbytes((ts, hw), BF16), 8 * tile)
    row_out = jax.ShapeDtypeStruct((batch, seq, hw), BF16)
    col_out = jax.ShapeDtypeStruct((batch, st, hw, ts), BF16)
    row_spec = pl.BlockSpec((1, ts, hw), lambda b, t: (b, t, 0))
    col_spec = pl.BlockSpec((1, 1, hw, ts), lambda b, t: (b, t, 0, 0))
    return pl.pallas_call(
        functools.partial(_qkv_body, head_dim=head_dim, scale=scale),
        out_shape=(col_out, col_out, row_out, row_out, col_out),
        grid=(batch, st),
        in_specs=[
            pl.BlockSpec((ts, d), lambda b, t: (b * st + t, 0)),
            pl.BlockSpec(wqT.shape, lambda b, t: (0, 0)),
            pl.BlockSpec(wk.shape, lambda b, t: (0, 0)),
            pl.BlockSpec(wvT.shape, lambda b, t: (0, 0)),
        ],
        out_specs=(col_spec, col_spec, row_spec, row_spec, col_spec),
        compiler_params=pltpu.CompilerParams(dimension_semantics=("arbitrary", "arbitrary"),
                                             vmem_limit_bytes=vmem),
        name="qkv_proj",
    )(x, wqT, wk, wvT)


def _attn_body(q1T_ref, q2T_ref, k1_ref, k2_ref, vT_ref, lam_ref, g_ref, o_ref,
               acc1, acc2, m1, l1, m2, l2, *, lambda_init):
    i = pl.program_id(2)
    ts = q1T_ref.shape[-1]
    q1T = q1T_ref[0, 0]
    q2T = q2T_ref[0, 0]
    for m, l, acc in ((m1, l1, acc1), (m2, l2, acc2)):
        m[...] = jnp.full(m.shape, -jnp.inf, F32)
        l[...] = jnp.zeros(l.shape, F32)
        acc[...] = jnp.zeros(acc.shape, F32)

    def one_map(k, qT, v, acc, m, l, visible):
        s = jnp.dot(k, qT, preferred_element_type=F32)
        if visible is not None:
            s = jnp.where(visible, s, -jnp.inf)
        m_old = m[...]
        m_new = jnp.maximum(m_old, jnp.max(s, axis=0, keepdims=True))
        alpha = jnp.exp(m_old - m_new)
        p = jnp.exp(s - m_new)
        l[...] = alpha * l[...] + jnp.sum(p, axis=0, keepdims=True)
        acc[...] = alpha * acc[...] + jnp.dot(v, p.astype(BF16), preferred_element_type=F32)
        m[...] = m_new

    def step(j, visible):
        rows = pl.ds(pl.multiple_of(j * ts, ts), ts)
        v = vT_ref[0, j]
        one_map(k1_ref[0, rows, :], q1T, v, acc1, m1, l1, visible)
        one_map(k2_ref[0, rows, :], q2T, v, acc2, m2, l2, visible)

    def full_block(j, carry):
        step(j, None)
        return carry

    lax.fori_loop(0, i, full_block, 0)
    key_idx = lax.broadcasted_iota(jnp.int32, (ts, ts), 0)
    qry_idx = lax.broadcasted_iota(jnp.int32, (ts, ts), 1)
    step(i, key_idx <= qry_idx)

    lam = (jnp.exp(jnp.sum(lam_ref[0:1, :] * lam_ref[1:2, :], axis=-1, keepdims=True))
           - jnp.exp(jnp.sum(lam_ref[2:3, :] * lam_ref[3:4, :], axis=-1, keepdims=True))
           + lambda_init)
    oT = acc1[...] / l1[...] - lam * (acc2[...] / l2[...])
    o = oT.T
    o = o * lax.rsqrt(jnp.mean(o * o, axis=-1, keepdims=True) + RMS_EPS)
    o = o * g_ref[...] * (1.0 - lambda_init)
    o_ref[0] = o.astype(o_ref.dtype)


def _diff_attn(q1T, q2T, k1, k2, vT, lam_vecs, subln_g, *, lambda_init):
    batch, st, hw, ts = q1T.shape
    seq = st * ts
    width = hw // N_HEADS
    resident = 2 * (2 * _nbytes((seq, width), BF16) + _nbytes((st, width, ts), BF16))
    vmem = _vmem_limit(resident, 2 * 3 * _nbytes((width, ts), BF16),
                       2 * _nbytes((width, ts), F32), 6 * _nbytes((ts, ts), F32))
    q_spec = pl.BlockSpec((1, 1, width, ts), lambda b, h, i: (b, i, h, 0))
    k_spec = pl.BlockSpec((1, seq, width), lambda b, h, i: (b, 0, h))
    stat = pltpu.VMEM((1, ts), F32)
    return pl.pallas_call(
        functools.partial(_attn_body, lambda_init=lambda_init),
        out_shape=jax.ShapeDtypeStruct((batch, seq, hw), BF16),
        grid=(batch, N_HEADS, st),
        in_specs=[
            q_spec, q_spec, k_spec, k_spec,
            pl.BlockSpec((1, st, width, ts), lambda b, h, i: (b, 0, h, 0)),
            pl.BlockSpec(lam_vecs.shape, lambda b, h, i: (0, 0)),
            pl.BlockSpec((1, width), lambda b, h, i: (0, 0)),
        ],
        out_specs=pl.BlockSpec((1, ts, width), lambda b, h, i: (b, i, h)),
        scratch_shapes=[pltpu.VMEM((width, ts), F32), pltpu.VMEM((width, ts), F32),
                        stat, stat, stat, stat],
        compiler_params=pltpu.CompilerParams(
            dimension_semantics=("arbitrary", "arbitrary", "arbitrary"), vmem_limit_bytes=vmem),
        name="diff_attn",
    )(q1T, q2T, k1, k2, vT, lam_vecs, subln_g)


def _proj_ln_body(x_ref, a_ref, w_ref, g_ref, b_ref, o_ref):
    y = jnp.dot(a_ref[...], w_ref[...], preferred_element_type=F32)
    o_ref[...] = _layer_norm(DEEPNORM_ALPHA * x_ref[...] + y, g_ref[...], b_ref[...])


def _proj_ln(x, a, w, g, b, *, tm=512):
    n, d = x.shape
    tile = _nbytes((tm, d), F32)
    vmem = _vmem_limit(_nbytes(w.shape, BF16), 4 * tile, tile, 3 * tile)
    return pl.pallas_call(
        _proj_ln_body,
        out_shape=jax.ShapeDtypeStruct((n, d), F32),
        grid=(n // tm,),
        in_specs=[
            pl.BlockSpec((tm, d), lambda i: (i, 0)),
            pl.BlockSpec((tm, a.shape[1]), lambda i: (i, 0)),
            pl.BlockSpec(w.shape, _const2, pipeline_mode=pl.Buffered(1)),
            pl.BlockSpec((1, d), _const2),
            pl.BlockSpec((1, d), _const2),
        ],
        out_specs=pl.BlockSpec((tm, d), lambda i: (i, 0)),
        compiler_params=pltpu.CompilerParams(dimension_semantics=("arbitrary",),
                                             vmem_limit_bytes=vmem),
        name="proj_ln",
    )(x, a, w, g, b)


def _row(v):
    return v.reshape(1, -1).astype(F32)


def _ffn(h, w_in, w_out, g, b):
    return _ffn_ln(h, w_in.astype(BF16), w_out.astype(BF16), _row(g), _row(b))


def kernel(x, l0_ffn1_w_in, l0_ffn1_w_out, l0_ln1_g, l0_ln1_b, l0_pool_w, l0_pool_scale, l0_ln2_g, l0_ln2_b, l0_ffn2_w_in, l0_ffn2_w_out, l0_ln3_g, l0_ln3_b, l1_ffn1_w_in, l1_ffn1_w_out, l1_ln1_g, l1_ln1_b, l1_w_qkv, l1_lam_q1, l1_lam_k1, l1_lam_q2, l1_lam_k2, l1_subln_g, l1_w_o, l1_ln2_g, l1_ln2_b, l1_ffn2_w_in, l1_ffn2_w_out, l1_ln3_g, l1_ln3_b):
    batch, seq, d = x.shape
    h = x.reshape(batch * seq, d)

    h = _ffn(h, l0_ffn1_w_in, l0_ffn1_w_out, l0_ln1_g, l0_ln1_b)
    h = _pool_ln(h, l0_pool_w.astype(BF16), _row(l0_pool_scale), _row(l0_ln2_g), _row(l0_ln2_b),
                 seq=seq)
    h = _ffn(h, l0_ffn2_w_in, l0_ffn2_w_out, l0_ln3_g, l0_ln3_b)

    h = _ffn(h, l1_ffn1_w_in, l1_ffn1_w_out, l1_ln1_g, l1_ln1_b)

    head_dim = d // (2 * N_HEADS)
    qk_w = N_HEADS * head_dim
    assert 2 * head_dim == V7X_LANES and l1_w_qkv.shape == (d, 4 * qk_w + N_HEADS * 2 * head_dim)

    def per_head_pairs(w_a, w_b):
        pair = jnp.stack([w_a.reshape(d, N_HEADS, head_dim), w_b.reshape(d, N_HEADS, head_dim)], 2)
        return pair.reshape(d, N_HEADS * 2 * head_dim)

    w = l1_w_qkv.astype(BF16)
    wqT = per_head_pairs(w[:, :qk_w], w[:, qk_w:2 * qk_w]).T
    wk = per_head_pairs(w[:, 2 * qk_w:3 * qk_w], w[:, 3 * qk_w:4 * qk_w])
    wvT = w[:, 4 * qk_w:].T
    q1T, q2T, k1, k2, vT = _qkv_proj(h, wqT, wk, wvT, batch=batch, seq=seq, ts=512,
                                     head_dim=head_dim, scale=head_dim ** -0.5)
    lambda_init = 0.8 - 0.6 * math.exp(-0.3 * 1)
    lam_vecs = jnp.stack([l1_lam_q1, l1_lam_k1, l1_lam_q2, l1_lam_k2]).astype(F32)
    attn = _diff_attn(q1T, q2T, k1, k2, vT, lam_vecs, _row(l1_subln_g), lambda_init=lambda_init)
    h = _proj_ln(h, attn.reshape(batch * seq, d), l1_w_o.astype(BF16), _row(l1_ln2_g),
                 _row(l1_ln2_b))

    h = _ffn(h, l1_ffn2_w_in, l1_ffn2_w_out, l1_ln3_g, l1_ln3_b)
    return h.reshape(batch, seq, d)
```

```python
import functools
import math

import jax
import jax.numpy as jnp
import ml_dtypes
from jax import lax
from jax.experimental import pallas as pl
from jax.experimental.pallas import tpu as pltpu

F32 = jnp.float32
BF16 = jnp.bfloat16

DEPTH = 2
FFN_RES = 0.5
POOL_WINDOWS = (2, 4, 8, 16)
POOL_HALO = 16
N_HEADS = 8
LN_EPS = 1e-5
RMS_EPS = 1e-5
DEEPNORM_ALPHA = (2.0 * DEPTH) ** 0.25

V7X_LANES = 128
V7X_VMEM_BYTES = 64 * 1024 * 1024
V7X_VMEM_RESERVE_BYTES = 8 * 1024 * 1024

POS_SPLIT_BITS = 6
MAX_SEQ = 1 << (POS_SPLIT_BITS + 8)


def _vmem_limit(*byte_counts):
    need = int(sum(byte_counts))
    return min(need + V7X_VMEM_RESERVE_BYTES, V7X_VMEM_BYTES - V7X_VMEM_RESERVE_BYTES)


def _nbytes(shape, dtype):
    return math.prod(shape) * jnp.dtype(dtype).itemsize


def _layer_norm(y, g, b):
    mu = jnp.mean(y, axis=-1, keepdims=True)
    yc = y - mu
    var = jnp.mean(yc * yc, axis=-1, keepdims=True)
    return yc * lax.rsqrt(var + LN_EPS) * g + b


def _const2(i):
    return (0, 0)


def _ffn_ln_body(x_ref, win_ref, wout_ref, g_ref, b_ref, o_ref, acc_ref, *, d_ff, chunk):
    x = x_ref[...]
    xb = x.astype(BF16)
    for c in range(d_ff // chunk):
        lo = c * chunk
        gate = jnp.dot(xb, win_ref[:, lo:lo + chunk], preferred_element_type=F32)
        up = jnp.dot(xb, win_ref[:, d_ff + lo:d_ff + lo + chunk], preferred_element_type=F32)
        act = (gate * jax.nn.sigmoid(gate) * up).astype(BF16)
        part = jnp.dot(act, wout_ref[lo:lo + chunk, :], preferred_element_type=F32)
        if c == 0:
            acc_ref[...] = part
        else:
            acc_ref[...] += part
    y = DEEPNORM_ALPHA * x + FFN_RES * acc_ref[...]
    o_ref[...] = _layer_norm(y, g_ref[...], b_ref[...])


def _ffn_ln(x, w_in, w_out, g, b, *, tm=512, chunk=256):
    n, d = x.shape
    d_ff = w_out.shape[0]
    assert n % tm == 0 and d_ff % chunk == 0 and w_in.shape == (d, 2 * d_ff)
    tile = _nbytes((tm, d), F32)
    vmem = _vmem_limit(_nbytes(w_in.shape, BF16), _nbytes(w_out.shape, BF16),
                       4 * tile,
                       tile,
                       3 * tile)
    return pl.pallas_call(
        functools.partial(_ffn_ln_body, d_ff=d_ff, chunk=chunk),
        out_shape=jax.ShapeDtypeStruct((n, d), F32),
        grid=(n // tm,),
        in_specs=[
            pl.BlockSpec((tm, d), lambda i: (i, 0)),
            pl.BlockSpec((d, 2 * d_ff), _const2, pipeline_mode=pl.Buffered(1)),
            pl.BlockSpec((d_ff, d), _const2, pipeline_mode=pl.Buffered(1)),
            pl.BlockSpec((1, d), _const2),
            pl.BlockSpec((1, d), _const2),
        ],
        out_specs=pl.BlockSpec((tm, d), lambda i: (i, 0)),
        scratch_shapes=[pltpu.VMEM((tm, d), F32)],
        compiler_params=pltpu.CompilerParams(dimension_semantics=("arbitrary",),
                                             vmem_limit_bytes=vmem),
        name="ffn_ln",
    )(x, w_in, w_out, g, b)


def _pool_ln_body(x_ref, halo_ref, w_ref, scale_ref, g_ref, b_ref, o_ref, ext_ref, *, seq_tiles):
    tm, d = x_ref.shape
    t = pl.program_id(0) % seq_tiles
    x = x_ref[...]
    ext_ref[0:POOL_HALO, :] = jnp.where(t == 0, 0.0, halo_ref[...])
    ext_ref[POOL_HALO:, :] = x
    pos = t * tm + lax.broadcasted_iota(jnp.int32, (tm, 1), 0)
    gw = d // len(POOL_WINDOWS)
    outs = []
    for gi, w in enumerate(POOL_WINDOWS):
        cols = slice(gi * gw, (gi + 1) * gw)
        wsum = x[:, cols]
        for k in range(1, w):
            wsum = wsum + ext_ref[POOL_HALO - k:POOL_HALO - k + tm, cols]
        count = jnp.minimum(pos + 1, w).astype(F32)
        dg = (wsum / count - x[:, cols]).astype(BF16)
        outs.append(jnp.dot(dg, w_ref[gi], preferred_element_type=F32))
    y = jnp.concatenate(outs, axis=-1) * scale_ref[...]
    o_ref[...] = _layer_norm(DEEPNORM_ALPHA * x + y, g_ref[...], b_ref[...])


def _pool_ln(x, w_pool, scale, g, b, *, seq, tm=512):
    n, d = x.shape
    assert seq % tm == 0 and tm % POOL_HALO == 0 and max(POOL_WINDOWS) <= POOL_HALO
    halo_blocks_per_tile = tm // POOL_HALO
    tile = _nbytes((tm, d), F32)
    vmem = _vmem_limit(2 * _nbytes(w_pool.shape, BF16), 4 * tile, tile, 4 * tile)
    return pl.pallas_call(
        functools.partial(_pool_ln_body, seq_tiles=seq // tm),
        out_shape=jax.ShapeDtypeStruct((n, d), F32),
        grid=(n // tm,),
        in_specs=[
            pl.BlockSpec((tm, d), lambda i: (i, 0)),
            pl.BlockSpec((POOL_HALO, d),
                         lambda i: (jnp.maximum(i * halo_blocks_per_tile - 1, 0), 0)),
            pl.BlockSpec(w_pool.shape, lambda i: (0, 0, 0)),
            pl.BlockSpec((1, d), _const2),
            pl.BlockSpec((1, d), _const2),
            pl.BlockSpec((1, d), _const2),
        ],
        out_specs=pl.BlockSpec((tm, d), lambda i: (i, 0)),
        scratch_shapes=[pltpu.VMEM((tm + POOL_HALO, d), F32)],
        compiler_params=pltpu.CompilerParams(dimension_semantics=("arbitrary",),
                                             vmem_limit_bytes=vmem),
        name="pool_ln",
    )(x, x, w_pool, scale, g, b)


_NT_DIMS = (((1,), (1,)), ((), ()))


def _bf16_pieces(value, n):
    pieces = []
    for _ in range(n):
        piece = float(ml_dtypes.bfloat16(value))
        pieces.append(piece)
        value -= piece
    return pieces


LOG2E_PIECES = _bf16_pieces(math.log2(math.e), 3)
N_BIAS = 2 * len(LOG2E_PIECES)
V_ROWS_PAD = 16


def _qkv_body(x_ref, wqT_ref, wk_ref, wvT_ref, q1T_ref, q2T_ref, k1_ref, k2_ref, vT_ref,
              *, head_dim, scale):
    t = pl.program_id(1)
    ts = x_ref.shape[0]
    width = 2 * head_dim
    n_heads = vT_ref.shape[2] // (width + V_ROWS_PAD)
    xb = x_ref[...].astype(BF16)
    qT = lax.dot_general(wqT_ref[...], xb, _NT_DIMS, preferred_element_type=F32)
    qT = qT * (scale * math.log2(math.e))
    vT = lax.dot_general(wvT_ref[...], xb, _NT_DIMS, preferred_element_type=F32)
    kk = jnp.dot(xb, wk_ref[...], preferred_element_type=F32)

    r = lax.broadcasted_iota(jnp.int32, qT.shape, 0) & (width - 1)

    def log2e_rows(rr):
        pair = rr >> 1
        return jnp.where(pair == 0, LOG2E_PIECES[0],
                         jnp.where(pair == 1, LOG2E_PIECES[1], LOG2E_PIECES[2]))

    q1T_ref[0, 0] = jnp.where(r < head_dim, qT,
                              jnp.where(r < head_dim + N_BIAS, log2e_rows(r - head_dim),
                                        0.0)).astype(BF16)
    q2T_ref[0, 0] = jnp.where(r >= head_dim, qT,
                              jnp.where(r < N_BIAS, log2e_rows(r), 0.0)).astype(BF16)

    ones_rows = (lax.broadcasted_iota(jnp.int32, (V_ROWS_PAD, ts), 0) == 0).astype(BF16)
    for h in range(n_heads):
        base = h * (width + V_ROWS_PAD)
        vT_ref[0, 0, base:base + width, :] = vT[h * width:(h + 1) * width, :].astype(BF16)
        vT_ref[0, 0, base + width:base + width + V_ROWS_PAD, :] = ones_rows

    col = lax.broadcasted_iota(jnp.int32, (1, kk.shape[1]), 1)
    c = col & (width - 1)
    slope = jnp.exp2(-((col // width) + 1).astype(F32))
    pos = t * ts + lax.broadcasted_iota(jnp.int32, (ts, 1), 0)
    hi = ((pos >> POS_SPLIT_BITS) << POS_SPLIT_BITS).astype(F32)
    lo = (pos & ((1 << POS_SPLIT_BITS) - 1)).astype(F32)
    bias = jnp.where((c & 1) == 0, slope * hi, slope * lo)
    k1_ref[0] = jnp.where(c < head_dim, kk,
                          jnp.where(c < head_dim + N_BIAS, bias, 0.0)).astype(BF16)
    k2_ref[0] = jnp.where(c >= head_dim, kk, jnp.where(c < N_BIAS, bias, 0.0)).astype(BF16)


def _qkv_proj(x, wqT, wk, wvT, *, batch, seq, ts, head_dim, scale):
    n, d = x.shape
    hw = wk.shape[1]
    width = 2 * head_dim
    n_heads = hw // width
    st = seq // ts
    assert seq % ts == 0 and seq <= MAX_SEQ and N_BIAS <= head_dim
    tile = _nbytes((ts, hw), F32)
    vmem = _vmem_limit(2 * 3 * _nbytes(wk.shape, BF16), 2 * _nbytes((ts, d), F32),
                       2 * 6 * _nbytes((ts, hw), BF16), 8 * tile)
    row_out = jax.ShapeDtypeStruct((batch, seq, hw), BF16)
    col_out = jax.ShapeDtypeStruct((batch, st, hw, ts), BF16)
    v_rows = n_heads * (width + V_ROWS_PAD)
    v_out = jax.ShapeDtypeStruct((batch, st, v_rows, ts), BF16)
    row_spec = pl.BlockSpec((1, ts, hw), lambda b, t: (b, t, 0))
    col_spec = pl.BlockSpec((1, 1, hw, ts), lambda b, t: (b, t, 0, 0))
    v_spec = pl.BlockSpec((1, 1, v_rows, ts), lambda b, t: (b, t, 0, 0))
    return pl.pallas_call(
        functools.partial(_qkv_body, head_dim=head_dim, scale=scale),
        out_shape=(col_out, col_out, row_out, row_out, v_out),
        grid=(batch, st),
        in_specs=[
            pl.BlockSpec((ts, d), lambda b, t: (b * st + t, 0)),
            pl.BlockSpec(wqT.shape, lambda b, t: (0, 0)),
            pl.BlockSpec(wk.shape, lambda b, t: (0, 0)),
            pl.BlockSpec(wvT.shape, lambda b, t: (0, 0)),
        ],
        out_specs=(col_spec, col_spec, row_spec, row_spec, v_spec),
        compiler_params=pltpu.CompilerParams(dimension_semantics=("arbitrary", "arbitrary"),
                                             vmem_limit_bytes=vmem),
        name="qkv_proj",
    )(x, wqT, wk, wvT)


def _attn_body(q1T_ref, q2T_ref, k1_ref, k2_ref, vT_ref, lam_ref, g_ref, o_ref,
               acc1, acc2, m1, m2, s_a, s_b, *, lambda_init):
    i = pl.program_id(2)
    ts = q1T_ref.shape[-1]
    dv = o_ref.shape[-1]
    q1T = q1T_ref[0, 0]
    q2T = q2T_ref[0, 0]
    for m, acc in ((m1, acc1), (m2, acc2)):
        m[...] = jnp.full(m.shape, -jnp.inf, F32)
        acc[...] = jnp.zeros(acc.shape, F32)

    def scores(j, s_buf):
        rows = pl.ds(pl.multiple_of(j * ts, ts), ts)
        s_buf[0] = jnp.dot(k1_ref[0, rows, :], q1T, preferred_element_type=F32)
        s_buf[1] = jnp.dot(k2_ref[0, rows, :], q2T, preferred_element_type=F32)

    def softmax_pv(j, s_buf, visible):
        v = vT_ref[0, j]
        for mp, (acc, m) in enumerate(((acc1, m1), (acc2, m2))):
            s = s_buf[mp]
            if visible is not None:
                s = jnp.where(visible, s, -jnp.inf)
            m_old = m[...]
            m_new = jnp.maximum(m_old, jnp.max(s, axis=0, keepdims=True))
            alpha = jnp.exp2(m_old - m_new)
            p = jnp.exp2(s - m_new).astype(BF16)
            acc[...] = alpha * acc[...] + jnp.dot(v, p, preferred_element_type=F32)
            m[...] = m_new

    def diagonal_mask():
        key_idx = lax.broadcasted_iota(jnp.int32, (ts, ts), 0)
        qry_idx = lax.broadcasted_iota(jnp.int32, (ts, ts), 1)
        return key_idx <= qry_idx

    scores(0, s_a)

    def block_pair(t, carry):
        j = 2 * t
        scores(j + 1, s_b)
        softmax_pv(j, s_a, None)
        scores(j + 2, s_a)
        softmax_pv(j + 1, s_b, None)
        return carry

    lax.fori_loop(0, lax.shift_right_logical(i, 1), block_pair, 0)
    odd = (i & 1) == 1

    @pl.when(odd)
    def _():
        scores(i, s_b)
        softmax_pv(i - 1, s_a, None)
        softmax_pv(i, s_b, diagonal_mask())

    @pl.when(jnp.logical_not(odd))
    def _():
        softmax_pv(i, s_a, diagonal_mask())

    lam = (jnp.exp(jnp.sum(lam_ref[0:1, :] * lam_ref[1:2, :], axis=-1, keepdims=True))
           - jnp.exp(jnp.sum(lam_ref[2:3, :] * lam_ref[3:4, :], axis=-1, keepdims=True))
           + lambda_init)
    o1T = acc1[0:dv, :] / acc1[dv:dv + 1, :]
    o2T = acc2[0:dv, :] / acc2[dv:dv + 1, :]
    o = (o1T - lam * o2T).T
    o = o * lax.rsqrt(jnp.mean(o * o, axis=-1, keepdims=True) + RMS_EPS)
    o = o * g_ref[...] * (1.0 - lambda_init)
    o_ref[0] = o.astype(o_ref.dtype)


def _diff_attn(q1T, q2T, k1, k2, vT, lam_vecs, subln_g, *, lambda_init):
    batch, st, hw, ts = q1T.shape
    seq = st * ts
    width = hw // N_HEADS
    v_rows = width + V_ROWS_PAD
    assert vT.shape == (batch, st, N_HEADS * v_rows, ts)
    resident = 2 * (2 * _nbytes((seq, width), BF16) + _nbytes((st, v_rows, ts), BF16))
    score_buf = pltpu.VMEM((2, ts, ts), F32)
    acc_buf = pltpu.VMEM((v_rows, ts), F32)
    stat = pltpu.VMEM((1, ts), F32)
    vmem = _vmem_limit(resident, 2 * 3 * _nbytes((width, ts), BF16),
                       2 * _nbytes((v_rows, ts), F32),
                       2 * _nbytes((2, ts, ts), F32),
                       4 * _nbytes((ts, ts), F32))
    q_spec = pl.BlockSpec((1, 1, width, ts), lambda b, h, i: (b, i, h, 0))
    k_spec = pl.BlockSpec((1, seq, width), lambda b, h, i: (b, 0, h))
    return pl.pallas_call(
        functools.partial(_attn_body, lambda_init=lambda_init),
        out_shape=jax.ShapeDtypeStruct((batch, seq, hw), BF16),
        grid=(batch, N_HEADS, st),
        in_specs=[
            q_spec, q_spec, k_spec, k_spec,
            pl.BlockSpec((1, st, v_rows, ts), lambda b, h, i: (b, 0, h, 0)),
            pl.BlockSpec(lam_vecs.shape, lambda b, h, i: (0, 0)),
            pl.BlockSpec((1, width), lambda b, h, i: (0, 0)),
        ],
        out_specs=pl.BlockSpec((1, ts, width), lambda b, h, i: (b, i, h)),
        scratch_shapes=[acc_buf, acc_buf, stat, stat, score_buf, score_buf],
        compiler_params=pltpu.CompilerParams(
            dimension_semantics=("arbitrary", "arbitrary", "arbitrary"), vmem_limit_bytes=vmem),
        name="diff_attn",
    )(q1T, q2T, k1, k2, vT, lam_vecs, subln_g)


def _proj_ln_body(x_ref, a_ref, w_ref, g_ref, b_ref, o_ref):
    y = jnp.dot(a_ref[...], w_ref[...], preferred_element_type=F32)
    o_ref[...] = _layer_norm(DEEPNORM_ALPHA * x_ref[...] + y, g_ref[...], b_ref[...])


def _proj_ln(x, a, w, g, b, *, tm=512):
    n, d = x.shape
    tile = _nbytes((tm, d), F32)
    vmem = _vmem_limit(_nbytes(w.shape, BF16), 4 * tile, tile, 3 * tile)
    return pl.pallas_call(
        _proj_ln_body,
        out_shape=jax.ShapeDtypeStruct((n, d), F32),
        grid=(n // tm,),
        in_specs=[
            pl.BlockSpec((tm, d), lambda i: (i, 0)),
            pl.BlockSpec((tm, a.shape[1]), lambda i: (i, 0)),
            pl.BlockSpec(w.shape, _const2, pipeline_mode=pl.Buffered(1)),
            pl.BlockSpec((1, d), _const2),
            pl.BlockSpec((1, d), _const2),
        ],
        out_specs=pl.BlockSpec((tm, d), lambda i: (i, 0)),
        compiler_params=pltpu.CompilerParams(dimension_semantics=("arbitrary",),
                                             vmem_limit_bytes=vmem),
        name="proj_ln",
    )(x, a, w, g, b)


def _row(v):
    return v.reshape(1, -1).astype(F32)


def _ffn(h, w_in, w_out, g, b):
    return _ffn_ln(h, w_in.astype(BF16), w_out.astype(BF16), _row(g), _row(b))


def kernel(x, l0_ffn1_w_in, l0_ffn1_w_out, l0_ln1_g, l0_ln1_b, l0_pool_w, l0_pool_scale, l0_ln2_g, l0_ln2_b, l0_ffn2_w_in, l0_ffn2_w_out, l0_ln3_g, l0_ln3_b, l1_ffn1_w_in, l1_ffn1_w_out, l1_ln1_g, l1_ln1_b, l1_w_qkv, l1_lam_q1, l1_lam_k1, l1_lam_q2, l1_lam_k2, l1_subln_g, l1_w_o, l1_ln2_g, l1_ln2_b, l1_ffn2_w_in, l1_ffn2_w_out, l1_ln3_g, l1_ln3_b):
    batch, seq, d = x.shape
    h = x.reshape(batch * seq, d)

    h = _ffn(h, l0_ffn1_w_in, l0_ffn1_w_out, l0_ln1_g, l0_ln1_b)
    h = _pool_ln(h, l0_pool_w.astype(BF16), _row(l0_pool_scale), _row(l0_ln2_g), _row(l0_ln2_b),
                 seq=seq)
    h = _ffn(h, l0_ffn2_w_in, l0_ffn2_w_out, l0_ln3_g, l0_ln3_b)

    h = _ffn(h, l1_ffn1_w_in, l1_ffn1_w_out, l1_ln1_g, l1_ln1_b)

    head_dim = d // (2 * N_HEADS)
    qk_w = N_HEADS * head_dim
    assert 2 * head_dim == V7X_LANES and l1_w_qkv.shape == (d, 4 * qk_w + N_HEADS * 2 * head_dim)

    def per_head_pairs(w_a, w_b):
        pair = jnp.stack([w_a.reshape(d, N_HEADS, head_dim), w_b.reshape(d, N_HEADS, head_dim)], 2)
        return pair.reshape(d, N_HEADS * 2 * head_dim)

    w = l1_w_qkv.astype(BF16)
    wqT = per_head_pairs(w[:, :qk_w], w[:, qk_w:2 * qk_w]).T
    wk = per_head_pairs(w[:, 2 * qk_w:3 * qk_w], w[:, 3 * qk_w:4 * qk_w])
    wvT = w[:, 4 * qk_w:].T
    q1T, q2T, k1, k2, vT = _qkv_proj(h, wqT, wk, wvT, batch=batch, seq=seq, ts=512,
                                     head_dim=head_dim, scale=head_dim ** -0.5)
    lambda_init = 0.8 - 0.6 * math.exp(-0.3 * 1)
    lam_vecs = jnp.stack([l1_lam_q1, l1_lam_k1, l1_lam_q2, l1_lam_k2]).astype(F32)
    attn = _diff_attn(q1T, q2T, k1, k2, vT, lam_vecs, _row(l1_subln_g), lambda_init=lambda_init)
    h = _proj_ln(h, attn.reshape(batch * seq, d), l1_w_o.astype(BF16), _row(l1_ln2_g),
                 _row(l1_ln2_b))

    h = _ffn(h, l1_ffn2_w_in, l1_ffn2_w_out, l1_ln3_g, l1_ln3_b)
    return h.reshape(batch, seq, d)
```

```python
import functools
import math

import jax
import jax.numpy as jnp
import ml_dtypes
from jax import lax
from jax.experimental import pallas as pl
from jax.experimental.pallas import tpu as pltpu

F32 = jnp.float32
BF16 = jnp.bfloat16

DEPTH = 2
FFN_RES = 0.5
POOL_WINDOWS = (2, 4, 8, 16)
POOL_HALO = 16
N_HEADS = 8
LN_EPS = 1e-5
RMS_EPS = 1e-5
DEEPNORM_ALPHA = (2.0 * DEPTH) ** 0.25

V7X_LANES = 128
V7X_VMEM_BYTES = 64 * 1024 * 1024
V7X_VMEM_RESERVE_BYTES = 8 * 1024 * 1024

POS_SPLIT_BITS = 6
MAX_SEQ = 1 << (POS_SPLIT_BITS + 8)


def _vmem_limit(*byte_counts):
    need = int(sum(byte_counts))
    return min(need + V7X_VMEM_RESERVE_BYTES, V7X_VMEM_BYTES - V7X_VMEM_RESERVE_BYTES)


def _nbytes(shape, dtype):
    return math.prod(shape) * jnp.dtype(dtype).itemsize


def _layer_norm(y, g, b):
    mu = jnp.mean(y, axis=-1, keepdims=True)
    yc = y - mu
    var = jnp.mean(yc * yc, axis=-1, keepdims=True)
    return yc * lax.rsqrt(var + LN_EPS) * g + b


def _const2(i):
    return (0, 0)


def _ffn_ln_body(x_ref, win_ref, wout_ref, g_ref, b_ref, o_ref, acc_ref, *, d_ff, chunk):
    x = x_ref[...]
    xb = x.astype(BF16)
    for c in range(d_ff // chunk):
        lo = c * chunk
        gate = jnp.dot(xb, win_ref[:, lo:lo + chunk], preferred_element_type=F32)
        up = jnp.dot(xb, win_ref[:, d_ff + lo:d_ff + lo + chunk], preferred_element_type=F32)
        act = (gate * jax.nn.sigmoid(gate) * up).astype(BF16)
        part = jnp.dot(act, wout_ref[lo:lo + chunk, :], preferred_element_type=F32)
        if c == 0:
            acc_ref[...] = part
        else:
            acc_ref[...] += part
    y = DEEPNORM_ALPHA * x + FFN_RES * acc_ref[...]
    o_ref[...] = _layer_norm(y, g_ref[...], b_ref[...])


def _ffn_ln(x, w_in, w_out, g, b, *, tm=512, chunk=256):
    n, d = x.shape
    d_ff = w_out.shape[0]
    assert n % tm == 0 and d_ff % chunk == 0 and w_in.shape == (d, 2 * d_ff)
    tile = _nbytes((tm, d), F32)
    vmem = _vmem_limit(_nbytes(w_in.shape, BF16), _nbytes(w_out.shape, BF16),
                       4 * tile,
                       tile,
                       3 * tile)
    return pl.pallas_call(
        functools.partial(_ffn_ln_body, d_ff=d_ff, chunk=chunk),
        out_shape=jax.ShapeDtypeStruct((n, d), F32),
        grid=(n // tm,),
        in_specs=[
            pl.BlockSpec((tm, d), lambda i: (i, 0)),
            pl.BlockSpec((d, 2 * d_ff), _const2, pipeline_mode=pl.Buffered(1)),
            pl.BlockSpec((d_ff, d), _const2, pipeline_mode=pl.Buffered(1)),
            pl.BlockSpec((1, d), _const2),
            pl.BlockSpec((1, d), _const2),
        ],
        out_specs=pl.BlockSpec((tm, d), lambda i: (i, 0)),
        scratch_shapes=[pltpu.VMEM((tm, d), F32)],
        compiler_params=pltpu.CompilerParams(dimension_semantics=("arbitrary",),
                                             vmem_limit_bytes=vmem),
        name="ffn_ln",
    )(x, w_in, w_out, g, b)


def _pool_ln_body(x_ref, halo_ref, w_ref, scale_ref, g_ref, b_ref, o_ref, ext_ref, *, seq_tiles):
    tm, d = x_ref.shape
    t = pl.program_id(0) % seq_tiles
    x = x_ref[...]
    ext_ref[0:POOL_HALO, :] = jnp.where(t == 0, 0.0, halo_ref[...])
    ext_ref[POOL_HALO:, :] = x
    pos = t * tm + lax.broadcasted_iota(jnp.int32, (tm, 1), 0)
    gw = d // len(POOL_WINDOWS)
    outs = []
    for gi, w in enumerate(POOL_WINDOWS):
        cols = slice(gi * gw, (gi + 1) * gw)
        wsum = x[:, cols]
        for k in range(1, w):
            wsum = wsum + ext_ref[POOL_HALO - k:POOL_HALO - k + tm, cols]
        count = jnp.minimum(pos + 1, w).astype(F32)
        dg = (wsum / count - x[:, cols]).astype(BF16)
        outs.append(jnp.dot(dg, w_ref[gi], preferred_element_type=F32))
    y = jnp.concatenate(outs, axis=-1) * scale_ref[...]
    o_ref[...] = _layer_norm(DEEPNORM_ALPHA * x + y, g_ref[...], b_ref[...])


def _pool_ln(x, w_pool, scale, g, b, *, seq, tm=512):
    n, d = x.shape
    assert seq % tm == 0 and tm % POOL_HALO == 0 and max(POOL_WINDOWS) <= POOL_HALO
    halo_blocks_per_tile = tm // POOL_HALO
    tile = _nbytes((tm, d), F32)
    vmem = _vmem_limit(2 * _nbytes(w_pool.shape, BF16), 4 * tile, tile, 4 * tile)
    return pl.pallas_call(
        functools.partial(_pool_ln_body, seq_tiles=seq // tm),
        out_shape=jax.ShapeDtypeStruct((n, d), F32),
        grid=(n // tm,),
        in_specs=[
            pl.BlockSpec((tm, d), lambda i: (i, 0)),
            pl.BlockSpec((POOL_HALO, d),
                         lambda i: (jnp.maximum(i * halo_blocks_per_tile - 1, 0), 0)),
            pl.BlockSpec(w_pool.shape, lambda i: (0, 0, 0)),
            pl.BlockSpec((1, d), _const2),
            pl.BlockSpec((1, d), _const2),
            pl.BlockSpec((1, d), _const2),
        ],
        out_specs=pl.BlockSpec((tm, d), lambda i: (i, 0)),
        scratch_shapes=[pltpu.VMEM((tm + POOL_HALO, d), F32)],
        compiler_params=pltpu.CompilerParams(dimension_semantics=("arbitrary",),
                                             vmem_limit_bytes=vmem),
        name="pool_ln",
    )(x, x, w_pool, scale, g, b)


_NT_DIMS = (((1,), (1,)), ((), ()))


def _bf16_pieces(value, n):
    pieces = []
    for _ in range(n):
        piece = float(ml_dtypes.bfloat16(value))
        pieces.append(piece)
        value -= piece
    return pieces


LOG2E_PIECES = _bf16_pieces(math.log2(math.e), 3)
N_BIAS = 2 * len(LOG2E_PIECES)
V_ROWS_PAD = 16


def _qkv_body(x_ref, wqT_ref, wk_ref, wvT_ref, q1T_ref, q2T_ref, k1_ref, k2_ref, vT_ref,
              *, head_dim, scale):
    t = pl.program_id(1)
    ts = x_ref.shape[0]
    width = 2 * head_dim
    n_heads = vT_ref.shape[2] // (width + V_ROWS_PAD)
    xb = x_ref[...].astype(BF16)
    qT = lax.dot_general(wqT_ref[...], xb, _NT_DIMS, preferred_element_type=F32)
    qT = qT * (scale * math.log2(math.e))
    vT = lax.dot_general(wvT_ref[...], xb, _NT_DIMS, preferred_element_type=F32)
    kk = jnp.dot(xb, wk_ref[...], preferred_element_type=F32)

    r = lax.broadcasted_iota(jnp.int32, qT.shape, 0) & (width - 1)

    def log2e_rows(rr):
        pair = rr >> 1
        return jnp.where(pair == 0, LOG2E_PIECES[0],
                         jnp.where(pair == 1, LOG2E_PIECES[1], LOG2E_PIECES[2]))

    q1T_ref[0, 0] = jnp.where(r < head_dim, qT,
                              jnp.where(r < head_dim + N_BIAS, log2e_rows(r - head_dim),
                                        0.0)).astype(BF16)
    q2T_ref[0, 0] = jnp.where(r >= head_dim, qT,
                              jnp.where(r < N_BIAS, log2e_rows(r), 0.0)).astype(BF16)

    ones_rows = (lax.broadcasted_iota(jnp.int32, (V_ROWS_PAD, ts), 0) == 0).astype(BF16)
    for h in range(n_heads):
        base = h * (width + V_ROWS_PAD)
        vT_ref[0, 0, base:base + width, :] = vT[h * width:(h + 1) * width, :].astype(BF16)
        vT_ref[0, 0, base + width:base + width + V_ROWS_PAD, :] = ones_rows

    col = lax.broadcasted_iota(jnp.int32, (1, kk.shape[1]), 1)
    c = col & (width - 1)
    slope = jnp.exp2(-((col // width) + 1).astype(F32))
    pos = t * ts + lax.broadcasted_iota(jnp.int32, (ts, 1), 0)
    hi = ((pos >> POS_SPLIT_BITS) << POS_SPLIT_BITS).astype(F32)
    lo = (pos & ((1 << POS_SPLIT_BITS) - 1)).astype(F32)
    bias = jnp.where((c & 1) == 0, slope * hi, slope * lo)
    k1_ref[0] = jnp.where(c < head_dim, kk,
                          jnp.where(c < head_dim + N_BIAS, bias, 0.0)).astype(BF16)
    k2_ref[0] = jnp.where(c >= head_dim, kk, jnp.where(c < N_BIAS, bias, 0.0)).astype(BF16)


def _qkv_proj(x, wqT, wk, wvT, *, batch, seq, ts, head_dim, scale):
    n, d = x.shape
    hw = wk.shape[1]
    width = 2 * head_dim
    n_heads = hw // width
    st = seq // ts
    assert seq % ts == 0 and seq <= MAX_SEQ and N_BIAS <= head_dim
    tile = _nbytes((ts, hw), F32)
    vmem = _vmem_limit(2 * 3 * _nbytes(wk.shape, BF16), 2 * _nbytes((ts, d), F32),
                       2 * 6 * _nbytes((ts, hw), BF16), 8 * tile)
    row_out = jax.ShapeDtypeStruct((batch, seq, hw), BF16)
    col_out = jax.ShapeDtypeStruct((batch, st, hw, ts), BF16)
    v_rows = n_heads * (width + V_ROWS_PAD)
    v_out = jax.ShapeDtypeStruct((batch, st, v_rows, ts), BF16)
    row_spec = pl.BlockSpec((1, ts, hw), lambda b, t: (b, t, 0))
    col_spec = pl.BlockSpec((1, 1, hw, ts), lambda b, t: (b, t, 0, 0))
    v_spec = pl.BlockSpec((1, 1, v_rows, ts), lambda b, t: (b, t, 0, 0))
    return pl.pallas_call(
        functools.partial(_qkv_body, head_dim=head_dim, scale=scale),
        out_shape=(col_out, col_out, row_out, row_out, v_out),
        grid=(batch, st),
        in_specs=[
            pl.BlockSpec((ts, d), lambda b, t: (b * st + t, 0)),
            pl.BlockSpec(wqT.shape, lambda b, t: (0, 0)),
            pl.BlockSpec(wk.shape, lambda b, t: (0, 0)),
            pl.BlockSpec(wvT.shape, lambda b, t: (0, 0)),
        ],
        out_specs=(col_spec, col_spec, row_spec, row_spec, v_spec),
        compiler_params=pltpu.CompilerParams(dimension_semantics=("arbitrary", "arbitrary"),
                                             vmem_limit_bytes=vmem),
        name="qkv_proj",
    )(x, wqT, wk, wvT)


def _attn_body(q1T_ref, q2T_ref, k1_ref, k2_ref, vT_ref, lam_ref, g_ref, o_ref,
               acc1, acc2, m1, m2, s_a, s_b, mx_a, mx_b, *, lambda_init):
    i = pl.program_id(2)
    ts = q1T_ref.shape[-1]
    tq = 2 * ts
    dv = o_ref.shape[-1]
    q_halves = ((q1T_ref[0, 0], q1T_ref[0, 1]), (q2T_ref[0, 0], q2T_ref[0, 1]))
    for m, acc in ((m1, acc1), (m2, acc2)):
        m[...] = jnp.full(m.shape, -jnp.inf, F32)
        acc[...] = jnp.zeros(acc.shape, F32)

    def scores(j, s_buf, mx_buf, halves=(0, 1)):
        rows = pl.ds(pl.multiple_of(j * ts, ts), ts)
        for mp, k_ref in enumerate((k1_ref, k2_ref)):
            k = k_ref[0, rows, :]
            for hf in halves:
                s = jnp.dot(k, q_halves[mp][hf], preferred_element_type=F32)
                s_buf[mp, :, hf * ts:(hf + 1) * ts] = s
                mx_buf[mp, :, hf * ts:(hf + 1) * ts] = jnp.max(s, axis=0, keepdims=True)

    def softmax_pv(j, s_buf, mx_buf, c0=0, c1=tq, causal=False):
        v = vT_ref[0, j]
        for mp, (acc, m) in enumerate(((acc1, m1), (acc2, m2))):
            s = s_buf[mp, :, c0:c1]
            if causal:
                key_idx = lax.broadcasted_iota(jnp.int32, s.shape, 0)
                qry_idx = lax.broadcasted_iota(jnp.int32, s.shape, 1)
                s = jnp.where(key_idx <= qry_idx, s, -jnp.inf)
                blk_max = jnp.max(s, axis=0, keepdims=True)
            else:
                blk_max = mx_buf[mp, :, c0:c1]
            m_old = m[:, c0:c1]
            m_new = jnp.maximum(m_old, blk_max)
            alpha = jnp.exp2(m_old - m_new)
            p = jnp.exp2(s - m_new).astype(BF16)
            acc[:, c0:c1] = alpha * acc[:, c0:c1] + jnp.dot(v, p, preferred_element_type=F32)
            m[:, c0:c1] = m_new

    scores(0, s_a, mx_a)

    def block_pair(t, carry):
        j = 2 * t
        scores(j + 1, s_b, mx_b)
        softmax_pv(j, s_a, mx_a)
        scores(j + 2, s_a, mx_a)
        softmax_pv(j + 1, s_b, mx_b)
        return carry

    lax.fori_loop(0, i, block_pair, 0)
    scores(2 * i + 1, s_b, mx_b, halves=(1,))
    softmax_pv(2 * i, s_a, mx_a, 0, ts, causal=True)
    softmax_pv(2 * i, s_a, mx_a, ts, tq)
    softmax_pv(2 * i + 1, s_b, mx_b, ts, tq, causal=True)

    lam = (jnp.exp(jnp.sum(lam_ref[0:1, :] * lam_ref[1:2, :], axis=-1, keepdims=True))
           - jnp.exp(jnp.sum(lam_ref[2:3, :] * lam_ref[3:4, :], axis=-1, keepdims=True))
           + lambda_init)
    o1T = acc1[0:dv, :] / acc1[dv:dv + 1, :]
    o2T = acc2[0:dv, :] / acc2[dv:dv + 1, :]
    o = (o1T - lam * o2T).T
    o = o * lax.rsqrt(jnp.mean(o * o, axis=-1, keepdims=True) + RMS_EPS)
    o = o * g_ref[...] * (1.0 - lambda_init)
    o_ref[0] = o.astype(o_ref.dtype)


def _diff_attn(q1T, q2T, k1, k2, vT, lam_vecs, subln_g, *, lambda_init):
    batch, st, hw, ts = q1T.shape
    seq = st * ts
    width = hw // N_HEADS
    v_rows = width + V_ROWS_PAD
    tq = 2 * ts
    assert vT.shape == (batch, st, N_HEADS * v_rows, ts) and st % 2 == 0
    resident = 2 * (2 * _nbytes((seq, width), BF16) + _nbytes((st, v_rows, ts), BF16))
    score_buf = pltpu.VMEM((2, ts, tq), F32)
    max_buf = pltpu.VMEM((2, 1, tq), F32)
    acc_buf = pltpu.VMEM((v_rows, tq), F32)
    stat = pltpu.VMEM((1, tq), F32)
    vmem = _vmem_limit(resident, 2 * 3 * _nbytes((width, tq), BF16),
                       2 * _nbytes((v_rows, tq), F32),
                       2 * _nbytes((2, ts, tq), F32),
                       4 * _nbytes((ts, tq), F32))
    q_spec = pl.BlockSpec((1, 2, width, ts), lambda b, h, i: (b, i, h, 0))
    k_spec = pl.BlockSpec((1, seq, width), lambda b, h, i: (b, 0, h))
    return pl.pallas_call(
        functools.partial(_attn_body, lambda_init=lambda_init),
        out_shape=jax.ShapeDtypeStruct((batch, seq, hw), BF16),
        grid=(batch, N_HEADS, st // 2),
        in_specs=[
            q_spec, q_spec, k_spec, k_spec,
            pl.BlockSpec((1, st, v_rows, ts), lambda b, h, i: (b, 0, h, 0)),
            pl.BlockSpec(lam_vecs.shape, lambda b, h, i: (0, 0)),
            pl.BlockSpec((1, width), lambda b, h, i: (0, 0)),
        ],
        out_specs=pl.BlockSpec((1, tq, width), lambda b, h, i: (b, i, h)),
        scratch_shapes=[acc_buf, acc_buf, stat, stat, score_buf, score_buf, max_buf, max_buf],
        compiler_params=pltpu.CompilerParams(
            dimension_semantics=("arbitrary", "arbitrary", "arbitrary"), vmem_limit_bytes=vmem),
        name="diff_attn",
    )(q1T, q2T, k1, k2, vT, lam_vecs, subln_g)


def _proj_ln_body(x_ref, a_ref, w_ref, g_ref, b_ref, o_ref):
    y = jnp.dot(a_ref[...], w_ref[...], preferred_element_type=F32)
    o_ref[...] = _layer_norm(DEEPNORM_ALPHA * x_ref[...] + y, g_ref[...], b_ref[...])


def _proj_ln(x, a, w, g, b, *, tm=512):
    n, d = x.shape
    tile = _nbytes((tm, d), F32)
    vmem = _vmem_limit(_nbytes(w.shape, BF16), 4 * tile, tile, 3 * tile)
    return pl.pallas_call(
        _proj_ln_body,
        out_shape=jax.ShapeDtypeStruct((n, d), F32),
        grid=(n // tm,),
        in_specs=[
            pl.BlockSpec((tm, d), lambda i: (i, 0)),
            pl.BlockSpec((tm, a.shape[1]), lambda i: (i, 0)),
            pl.BlockSpec(w.shape, _const2, pipeline_mode=pl.Buffered(1)),
            pl.BlockSpec((1, d), _const2),
            pl.BlockSpec((1, d), _const2),
        ],
        out_specs=pl.BlockSpec((tm, d), lambda i: (i, 0)),
        compiler_params=pltpu.CompilerParams(dimension_semantics=("arbitrary",),
                                             vmem_limit_bytes=vmem),
        name="proj_ln",
    )(x, a, w, g, b)


def _row(v):
    return v.reshape(1, -1).astype(F32)


def _ffn(h, w_in, w_out, g, b):
    return _ffn_ln(h, w_in.astype(BF16), w_out.astype(BF16), _row(g), _row(b))


def kernel(x, l0_ffn1_w_in, l0_ffn1_w_out, l0_ln1_g, l0_ln1_b, l0_pool_w, l0_pool_scale, l0_ln2_g, l0_ln2_b, l0_ffn2_w_in, l0_ffn2_w_out, l0_ln3_g, l0_ln3_b, l1_ffn1_w_in, l1_ffn1_w_out, l1_ln1_g, l1_ln1_b, l1_w_qkv, l1_lam_q1, l1_lam_k1, l1_lam_q2, l1_lam_k2, l1_subln_g, l1_w_o, l1_ln2_g, l1_ln2_b, l1_ffn2_w_in, l1_ffn2_w_out, l1_ln3_g, l1_ln3_b):
    batch, seq, d = x.shape
    h = x.reshape(batch * seq, d)

    h = _ffn(h, l0_ffn1_w_in, l0_ffn1_w_out, l0_ln1_g, l0_ln1_b)
    h = _pool_ln(h, l0_pool_w.astype(BF16), _row(l0_pool_scale), _row(l0_ln2_g), _row(l0_ln2_b),
                 seq=seq)
    h = _ffn(h, l0_ffn2_w_in, l0_ffn2_w_out, l0_ln3_g, l0_ln3_b)

    h = _ffn(h, l1_ffn1_w_in, l1_ffn1_w_out, l1_ln1_g, l1_ln1_b)

    head_dim = d // (2 * N_HEADS)
    qk_w = N_HEADS * head_dim
    assert 2 * head_dim == V7X_LANES and l1_w_qkv.shape == (d, 4 * qk_w + N_HEADS * 2 * head_dim)

    def per_head_pairs(w_a, w_b):
        pair = jnp.stack([w_a.reshape(d, N_HEADS, head_dim), w_b.reshape(d, N_HEADS, head_dim)], 2)
        return pair.reshape(d, N_HEADS * 2 * head_dim)

    w = l1_w_qkv.astype(BF16)
    wqT = per_head_pairs(w[:, :qk_w], w[:, qk_w:2 * qk_w]).T
    wk = per_head_pairs(w[:, 2 * qk_w:3 * qk_w], w[:, 3 * qk_w:4 * qk_w])
    wvT = w[:, 4 * qk_w:].T
    q1T, q2T, k1, k2, vT = _qkv_proj(h, wqT, wk, wvT, batch=batch, seq=seq, ts=512,
                                     head_dim=head_dim, scale=head_dim ** -0.5)
    lambda_init = 0.8 - 0.6 * math.exp(-0.3 * 1)
    lam_vecs = jnp.stack([l1_lam_q1, l1_lam_k1, l1_lam_q2, l1_lam_k2]).astype(F32)
    attn = _diff_attn(q1T, q2T, k1, k2, vT, lam_vecs, _row(l1_subln_g), lambda_init=lambda_init)
    h = _proj_ln(h, attn.reshape(batch * seq, d), l1_w_o.astype(BF16), _row(l1_ln2_g),
                 _row(l1_ln2_b))

    h = _ffn(h, l1_ffn2_w_in, l1_ffn2_w_out, l1_ln3_g, l1_ln3_b)
    return h.reshape(batch, seq, d)
```

```python
import functools
import math

import jax
import jax.numpy as jnp
import ml_dtypes
from jax import lax
from jax.experimental import pallas as pl
from jax.experimental.pallas import tpu as pltpu

F32 = jnp.float32
BF16 = jnp.bfloat16

DEPTH = 2
FFN_RES = 0.5
POOL_WINDOWS = (2, 4, 8, 16)
POOL_HALO = 16
N_HEADS = 8
LN_EPS = 1e-5
RMS_EPS = 1e-5
DEEPNORM_ALPHA = (2.0 * DEPTH) ** 0.25

V7X_LANES = 128
V7X_VMEM_BYTES = 64 * 1024 * 1024
V7X_VMEM_RESERVE_BYTES = 8 * 1024 * 1024

POS_SPLIT_BITS = 6
MAX_SEQ = 1 << (POS_SPLIT_BITS + 8)


def _vmem_limit(*byte_counts):
    need = int(sum(byte_counts))
    return min(need + V7X_VMEM_RESERVE_BYTES, V7X_VMEM_BYTES - V7X_VMEM_RESERVE_BYTES)


def _nbytes(shape, dtype):
    return math.prod(shape) * jnp.dtype(dtype).itemsize


def _layer_norm(y, g, b):
    mu = jnp.mean(y, axis=-1, keepdims=True)
    yc = y - mu
    var = jnp.mean(yc * yc, axis=-1, keepdims=True)
    return yc * lax.rsqrt(var + LN_EPS) * g + b


def _const2(i):
    return (0, 0)


def _ffn_ln_body(x_ref, win_ref, wout_ref, g_ref, b_ref, o_ref, acc_ref, *, d_ff, chunk):
    x = x_ref[...]
    xb = x.astype(BF16)
    for c in range(d_ff // chunk):
        lo = c * chunk
        gate = jnp.dot(xb, win_ref[:, lo:lo + chunk], preferred_element_type=F32)
        up = jnp.dot(xb, win_ref[:, d_ff + lo:d_ff + lo + chunk], preferred_element_type=F32)
        act = (gate * jax.nn.sigmoid(gate) * up).astype(BF16)
        part = jnp.dot(act, wout_ref[lo:lo + chunk, :], preferred_element_type=F32)
        if c == 0:
            acc_ref[...] = part
        else:
            acc_ref[...] += part
    y = DEEPNORM_ALPHA * x + FFN_RES * acc_ref[...]
    o_ref[...] = _layer_norm(y, g_ref[...], b_ref[...])


def _ffn_ln(x, w_in, w_out, g, b, *, tm=512, chunk=256):
    n, d = x.shape
    d_ff = w_out.shape[0]
    assert n % tm == 0 and d_ff % chunk == 0 and w_in.shape == (d, 2 * d_ff)
    tile = _nbytes((tm, d), F32)
    vmem = _vmem_limit(_nbytes(w_in.shape, BF16), _nbytes(w_out.shape, BF16),
                       4 * tile,
                       tile,
                       3 * tile)
    return pl.pallas_call(
        functools.partial(_ffn_ln_body, d_ff=d_ff, chunk=chunk),
        out_shape=jax.ShapeDtypeStruct((n, d), F32),
        grid=(n // tm,),
        in_specs=[
            pl.BlockSpec((tm, d), lambda i: (i, 0)),
            pl.BlockSpec((d, 2 * d_ff), _const2, pipeline_mode=pl.Buffered(1)),
            pl.BlockSpec((d_ff, d), _const2, pipeline_mode=pl.Buffered(1)),
            pl.BlockSpec((1, d), _const2),
            pl.BlockSpec((1, d), _const2),
        ],
        out_specs=pl.BlockSpec((tm, d), lambda i: (i, 0)),
        scratch_shapes=[pltpu.VMEM((tm, d), F32)],
        compiler_params=pltpu.CompilerParams(dimension_semantics=("arbitrary",),
                                             vmem_limit_bytes=vmem),
        name="ffn_ln",
    )(x, w_in, w_out, g, b)


def _pool_ln_body(x_ref, halo_ref, w_ref, scale_ref, g_ref, b_ref, o_ref, ext_ref, *, seq_tiles):
    tm, d = x_ref.shape
    t = pl.program_id(0) % seq_tiles
    x = x_ref[...]
    ext_ref[0:POOL_HALO, :] = jnp.where(t == 0, 0.0, halo_ref[...])
    ext_ref[POOL_HALO:, :] = x
    pos = t * tm + lax.broadcasted_iota(jnp.int32, (tm, 1), 0)
    gw = d // len(POOL_WINDOWS)
    outs = []
    for gi, w in enumerate(POOL_WINDOWS):
        cols = slice(gi * gw, (gi + 1) * gw)
        wsum = x[:, cols]
        for k in range(1, w):
            wsum = wsum + ext_ref[POOL_HALO - k:POOL_HALO - k + tm, cols]
        count = jnp.minimum(pos + 1, w).astype(F32)
        dg = (wsum / count - x[:, cols]).astype(BF16)
        outs.append(jnp.dot(dg, w_ref[gi], preferred_element_type=F32))
    y = jnp.concatenate(outs, axis=-1) * scale_ref[...]
    o_ref[...] = _layer_norm(DEEPNORM_ALPHA * x + y, g_ref[...], b_ref[...])


def _pool_ln(x, w_pool, scale, g, b, *, seq, tm=512):
    n, d = x.shape
    assert seq % tm == 0 and tm % POOL_HALO == 0 and max(POOL_WINDOWS) <= POOL_HALO
    halo_blocks_per_tile = tm // POOL_HALO
    tile = _nbytes((tm, d), F32)
    vmem = _vmem_limit(2 * _nbytes(w_pool.shape, BF16), 4 * tile, tile, 4 * tile)
    return pl.pallas_call(
        functools.partial(_pool_ln_body, seq_tiles=seq // tm),
        out_shape=jax.ShapeDtypeStruct((n, d), F32),
        grid=(n // tm,),
        in_specs=[
            pl.BlockSpec((tm, d), lambda i: (i, 0)),
            pl.BlockSpec((POOL_HALO, d),
                         lambda i: (jnp.maximum(i * halo_blocks_per_tile - 1, 0), 0)),
            pl.BlockSpec(w_pool.shape, lambda i: (0, 0, 0)),
            pl.BlockSpec((1, d), _const2),
            pl.BlockSpec((1, d), _const2),
            pl.BlockSpec((1, d), _const2),
        ],
        out_specs=pl.BlockSpec((tm, d), lambda i: (i, 0)),
        scratch_shapes=[pltpu.VMEM((tm + POOL_HALO, d), F32)],
        compiler_params=pltpu.CompilerParams(dimension_semantics=("arbitrary",),
                                             vmem_limit_bytes=vmem),
        name="pool_ln",
    )(x, x, w_pool, scale, g, b)


_NT_DIMS = (((1,), (1,)), ((), ()))


def _bf16_pieces(value, n):
    pieces = []
    for _ in range(n):
        piece = float(ml_dtypes.bfloat16(value))
        pieces.append(piece)
        value -= piece
    return pieces


LOG2E_PIECES = _bf16_pieces(math.log2(math.e), 3)
N_BIAS = 2 * len(LOG2E_PIECES)
V_ROWS_PAD = 16


def _qkv_body(x_ref, wqT_ref, wk_ref, wvT_ref, q1T_ref, q2T_ref, k1_ref, k2_ref, vT_ref,
              *, head_dim, scale):
    t = pl.program_id(1)
    ts = x_ref.shape[0]
    width = 2 * head_dim
    n_heads = vT_ref.shape[2] // (width + V_ROWS_PAD)
    xb = x_ref[...].astype(BF16)
    qT = lax.dot_general(wqT_ref[...], xb, _NT_DIMS, preferred_element_type=F32)
    qT = qT * (scale * math.log2(math.e))
    vT = lax.dot_general(wvT_ref[...], xb, _NT_DIMS, preferred_element_type=F32)
    kk = jnp.dot(xb, wk_ref[...], preferred_element_type=F32)

    r = lax.broadcasted_iota(jnp.int32, qT.shape, 0) & (width - 1)

    def log2e_rows(rr):
        pair = rr >> 1
        return jnp.where(pair == 0, LOG2E_PIECES[0],
                         jnp.where(pair == 1, LOG2E_PIECES[1], LOG2E_PIECES[2]))

    q1T_ref[0, 0] = jnp.where(r < head_dim, qT,
                              jnp.where(r < head_dim + N_BIAS, log2e_rows(r - head_dim),
                                        0.0)).astype(BF16)
    q2T_ref[0, 0] = jnp.where(r >= head_dim, qT,
                              jnp.where(r < N_BIAS, log2e_rows(r), 0.0)).astype(BF16)

    ones_rows = (lax.broadcasted_iota(jnp.int32, (V_ROWS_PAD, ts), 0) == 0).astype(BF16)
    for h in range(n_heads):
        base = h * (width + V_ROWS_PAD)
        vT_ref[0, 0, base:base + width, :] = vT[h * width:(h + 1) * width, :].astype(BF16)
        vT_ref[0, 0, base + width:base + width + V_ROWS_PAD, :] = ones_rows

    col = lax.broadcasted_iota(jnp.int32, (1, kk.shape[1]), 1)
    c = col & (width - 1)
    slope = jnp.exp2(-((col // width) + 1).astype(F32))
    pos = t * ts + lax.broadcasted_iota(jnp.int32, (ts, 1), 0)
    hi = ((pos >> POS_SPLIT_BITS) << POS_SPLIT_BITS).astype(F32)
    lo = (pos & ((1 << POS_SPLIT_BITS) - 1)).astype(F32)
    bias = jnp.where((c & 1) == 0, slope * hi, slope * lo)
    k1_ref[0] = jnp.where(c < head_dim, kk,
                          jnp.where(c < head_dim + N_BIAS, bias, 0.0)).astype(BF16)
    k2_ref[0] = jnp.where(c >= head_dim, kk, jnp.where(c < N_BIAS, bias, 0.0)).astype(BF16)


def _qkv_proj(x, wqT, wk, wvT, *, batch, seq, ts, head_dim, scale):
    n, d = x.shape
    hw = wk.shape[1]
    width = 2 * head_dim
    n_heads = hw // width
    st = seq // ts
    assert seq % ts == 0 and seq <= MAX_SEQ and N_BIAS <= head_dim
    tile = _nbytes((ts, hw), F32)
    vmem = _vmem_limit(2 * 3 * _nbytes(wk.shape, BF16), 2 * _nbytes((ts, d), F32),
                       2 * 6 * _nbytes((ts, hw), BF16), 8 * tile)
    row_out = jax.ShapeDtypeStruct((batch, seq, hw), BF16)
    col_out = jax.ShapeDtypeStruct((batch, st, hw, ts), BF16)
    v_rows = n_heads * (width + V_ROWS_PAD)
    v_out = jax.ShapeDtypeStruct((batch, st, v_rows, ts), BF16)
    row_spec = pl.BlockSpec((1, ts, hw), lambda b, t: (b, t, 0))
    col_spec = pl.BlockSpec((1, 1, hw, ts), lambda b, t: (b, t, 0, 0))
    v_spec = pl.BlockSpec((1, 1, v_rows, ts), lambda b, t: (b, t, 0, 0))
    return pl.pallas_call(
        functools.partial(_qkv_body, head_dim=head_dim, scale=scale),
        out_shape=(col_out, col_out, row_out, row_out, v_out),
        grid=(batch, st),
        in_specs=[
            pl.BlockSpec((ts, d), lambda b, t: (b * st + t, 0)),
            pl.BlockSpec(wqT.shape, lambda b, t: (0, 0)),
            pl.BlockSpec(wk.shape, lambda b, t: (0, 0)),
            pl.BlockSpec(wvT.shape, lambda b, t: (0, 0)),
        ],
        out_specs=(col_spec, col_spec, row_spec, row_spec, v_spec),
        compiler_params=pltpu.CompilerParams(dimension_semantics=("arbitrary", "arbitrary"),
                                             vmem_limit_bytes=vmem),
        name="qkv_proj",
    )(x, wqT, wk, wvT)


def _attn_body(q1T_ref, q2T_ref, k1_ref, k2_ref, vT_ref, lam_ref, g_ref, o_ref,
               acc1, acc2, m1, m2, s_a, s_b, mx_a, mx_b, *, lambda_init):
    ts = q1T_ref.shape[-1]
    tq = 2 * ts
    n_tiles = q1T_ref.shape[1] // 2
    dv = o_ref.shape[-1]
    units = tuple((mp, hf) for mp in (0, 1) for hf in (0, 1))

    lam = (jnp.exp(jnp.sum(lam_ref[0:1, :] * lam_ref[1:2, :], axis=-1, keepdims=True))
           - jnp.exp(jnp.sum(lam_ref[2:3, :] * lam_ref[3:4, :], axis=-1, keepdims=True))
           + lambda_init)

    def scores(i, j, s_buf, mx_buf, mp, hf):
        rows = pl.ds(pl.multiple_of(j * ts, ts), ts)
        k = (k1_ref, k2_ref)[mp][0, rows, :]
        qT = (q1T_ref, q2T_ref)[mp][0, 2 * i + hf]
        s = jnp.dot(k, qT, preferred_element_type=F32)
        s_buf[mp, :, hf * ts:(hf + 1) * ts] = s
        mx_buf[mp, :, hf * ts:(hf + 1) * ts] = jnp.max(s, axis=0, keepdims=True)

    def softmax_pv(j, s_buf, mx_buf, mp, hf, causal=False):
        acc, m = ((acc1, m1), (acc2, m2))[mp]
        cols = slice(hf * ts, (hf + 1) * ts)
        v = vT_ref[0, j]
        s = s_buf[mp, :, cols]
        if causal:
            key_idx = lax.broadcasted_iota(jnp.int32, s.shape, 0)
            qry_idx = lax.broadcasted_iota(jnp.int32, s.shape, 1)
            s = jnp.where(key_idx <= qry_idx, s, -jnp.inf)
            blk_max = jnp.max(s, axis=0, keepdims=True)
        else:
            blk_max = mx_buf[mp, :, cols]
        m_old = m[:, cols]
        m_new = jnp.maximum(m_old, blk_max)
        alpha = jnp.exp2(m_old - m_new)
        p = jnp.exp2(s - m_new).astype(BF16)
        acc[:, cols] = alpha * acc[:, cols] + jnp.dot(v, p, preferred_element_type=F32)
        m[:, cols] = m_new

    for mp, hf in units:
        scores(0, 0, s_a, mx_a, mp, hf)

    def query_tile(i, carry):
        for m, acc in ((m1, acc1), (m2, acc2)):
            m[...] = jnp.full(m.shape, -jnp.inf, F32)
            acc[...] = jnp.zeros(acc.shape, F32)

        def block_pair(t, inner):
            j = 2 * t
            for mp, hf in units:
                scores(i, j + 1, s_b, mx_b, mp, hf)
                softmax_pv(j, s_a, mx_a, mp, hf)
            for mp, hf in units:
                scores(i, j + 2, s_a, mx_a, mp, hf)
                softmax_pv(j + 1, s_b, mx_b, mp, hf)
            return inner

        lax.fori_loop(0, i, block_pair, 0)
        for mp in (0, 1):
            scores(i, 2 * i + 1, s_b, mx_b, mp, 1)
            softmax_pv(2 * i, s_a, mx_a, mp, 0, causal=True)
            softmax_pv(2 * i, s_a, mx_a, mp, 1)
        nxt = jnp.minimum(i + 1, n_tiles - 1)
        for mp in (0, 1):
            softmax_pv(2 * i + 1, s_b, mx_b, mp, 1, causal=True)
            for hf in (0, 1):
                scores(nxt, 0, s_a, mx_a, mp, hf)

        o1T = acc1[0:dv, :] / acc1[dv:dv + 1, :]
        o2T = acc2[0:dv, :] / acc2[dv:dv + 1, :]
        o = (o1T - lam * o2T).T
        o = o * lax.rsqrt(jnp.mean(o * o, axis=-1, keepdims=True) + RMS_EPS)
        o = o * g_ref[...] * (1.0 - lambda_init)
        o_ref[0, pl.ds(pl.multiple_of(i * tq, tq), tq), :] = o.astype(o_ref.dtype)
        return carry

    lax.fori_loop(0, n_tiles, query_tile, 0)


def _diff_attn(q1T, q2T, k1, k2, vT, lam_vecs, subln_g, *, lambda_init):
    batch, st, hw, ts = q1T.shape
    seq = st * ts
    width = hw // N_HEADS
    v_rows = width + V_ROWS_PAD
    tq = 2 * ts
    assert vT.shape == (batch, st, N_HEADS * v_rows, ts) and st % 2 == 0
    resident = 2 * (5 * _nbytes((seq, width), BF16) + _nbytes((st, v_rows, ts), BF16))
    score_buf = pltpu.VMEM((2, ts, tq), F32)
    max_buf = pltpu.VMEM((2, 1, tq), F32)
    acc_buf = pltpu.VMEM((v_rows, tq), F32)
    stat = pltpu.VMEM((1, tq), F32)
    vmem = _vmem_limit(resident,
                       2 * _nbytes((v_rows, tq), F32),
                       2 * _nbytes((2, ts, tq), F32),
                       4 * _nbytes((ts, tq), F32))
    q_spec = pl.BlockSpec((1, st, width, ts), lambda b, h: (b, 0, h, 0))
    k_spec = pl.BlockSpec((1, seq, width), lambda b, h: (b, 0, h))
    return pl.pallas_call(
        functools.partial(_attn_body, lambda_init=lambda_init),
        out_shape=jax.ShapeDtypeStruct((batch, seq, hw), BF16),
        grid=(batch, N_HEADS),
        in_specs=[
            q_spec, q_spec, k_spec, k_spec,
            pl.BlockSpec((1, st, v_rows, ts), lambda b, h: (b, 0, h, 0)),
            pl.BlockSpec(lam_vecs.shape, lambda b, h: (0, 0)),
            pl.BlockSpec((1, width), lambda b, h: (0, 0)),
        ],
        out_specs=pl.BlockSpec((1, seq, width), lambda b, h: (b, 0, h)),
        scratch_shapes=[acc_buf, acc_buf, stat, stat, score_buf, score_buf, max_buf, max_buf],
        compiler_params=pltpu.CompilerParams(
            dimension_semantics=("arbitrary", "arbitrary"), vmem_limit_bytes=vmem),
        name="diff_attn",
    )(q1T, q2T, k1, k2, vT, lam_vecs, subln_g)


def _proj_ln_body(x_ref, a_ref, w_ref, g_ref, b_ref, o_ref):
    y = jnp.dot(a_ref[...], w_ref[...], preferred_element_type=F32)
    o_ref[...] = _layer_norm(DEEPNORM_ALPHA * x_ref[...] + y, g_ref[...], b_ref[...])


def _proj_ln(x, a, w, g, b, *, tm=512):
    n, d = x.shape
    tile = _nbytes((tm, d), F32)
    vmem = _vmem_limit(_nbytes(w.shape, BF16), 4 * tile, tile, 3 * tile)
    return pl.pallas_call(
        _proj_ln_body,
        out_shape=jax.ShapeDtypeStruct((n, d), F32),
        grid=(n // tm,),
        in_specs=[
            pl.BlockSpec((tm, d), lambda i: (i, 0)),
            pl.BlockSpec((tm, a.shape[1]), lambda i: (i, 0)),
            pl.BlockSpec(w.shape, _const2, pipeline_mode=pl.Buffered(1)),
            pl.BlockSpec((1, d), _const2),
            pl.BlockSpec((1, d), _const2),
        ],
        out_specs=pl.BlockSpec((tm, d), lambda i: (i, 0)),
        compiler_params=pltpu.CompilerParams(dimension_semantics=("arbitrary",),
                                             vmem_limit_bytes=vmem),
        name="proj_ln",
    )(x, a, w, g, b)


def _row(v):
    return v.reshape(1, -1).astype(F32)


def _ffn(h, w_in, w_out, g, b):
    return _ffn_ln(h, w_in.astype(BF16), w_out.astype(BF16), _row(g), _row(b))


def kernel(x, l0_ffn1_w_in, l0_ffn1_w_out, l0_ln1_g, l0_ln1_b, l0_pool_w, l0_pool_scale, l0_ln2_g, l0_ln2_b, l0_ffn2_w_in, l0_ffn2_w_out, l0_ln3_g, l0_ln3_b, l1_ffn1_w_in, l1_ffn1_w_out, l1_ln1_g, l1_ln1_b, l1_w_qkv, l1_lam_q1, l1_lam_k1, l1_lam_q2, l1_lam_k2, l1_subln_g, l1_w_o, l1_ln2_g, l1_ln2_b, l1_ffn2_w_in, l1_ffn2_w_out, l1_ln3_g, l1_ln3_b):
    batch, seq, d = x.shape
    h = x.reshape(batch * seq, d)

    h = _ffn(h, l0_ffn1_w_in, l0_ffn1_w_out, l0_ln1_g, l0_ln1_b)
    h = _pool_ln(h, l0_pool_w.astype(BF16), _row(l0_pool_scale), _row(l0_ln2_g), _row(l0_ln2_b),
                 seq=seq)
    h = _ffn(h, l0_ffn2_w_in, l0_ffn2_w_out, l0_ln3_g, l0_ln3_b)

    h = _ffn(h, l1_ffn1_w_in, l1_ffn1_w_out, l1_ln1_g, l1_ln1_b)

    head_dim = d // (2 * N_HEADS)
    qk_w = N_HEADS * head_dim
    assert 2 * head_dim == V7X_LANES and l1_w_qkv.shape == (d, 4 * qk_w + N_HEADS * 2 * head_dim)

    def per_head_pairs(w_a, w_b):
        pair = jnp.stack([w_a.reshape(d, N_HEADS, head_dim), w_b.reshape(d, N_HEADS, head_dim)], 2)
        return pair.reshape(d, N_HEADS * 2 * head_dim)

    w = l1_w_qkv.astype(BF16)
    wqT = per_head_pairs(w[:, :qk_w], w[:, qk_w:2 * qk_w]).T
    wk = per_head_pairs(w[:, 2 * qk_w:3 * qk_w], w[:, 3 * qk_w:4 * qk_w])
    wvT = w[:, 4 * qk_w:].T
    q1T, q2T, k1, k2, vT = _qkv_proj(h, wqT, wk, wvT, batch=batch, seq=seq, ts=512,
                                     head_dim=head_dim, scale=head_dim ** -0.5)
    lambda_init = 0.8 - 0.6 * math.exp(-0.3 * 1)
    lam_vecs = jnp.stack([l1_lam_q1, l1_lam_k1, l1_lam_q2, l1_lam_k2]).astype(F32)
    attn = _diff_attn(q1T, q2T, k1, k2, vT, lam_vecs, _row(l1_subln_g), lambda_init=lambda_init)
    h = _proj_ln(h, attn.reshape(batch * seq, d), l1_w_o.astype(BF16), _row(l1_ln2_g),
                 _row(l1_ln2_b))

    h = _ffn(h, l1_ffn2_w_in, l1_ffn2_w_out, l1_ln3_g, l1_ln3_b)
    return h.reshape(batch, seq, d)
```

```python
import functools
import math

import jax
import jax.numpy as jnp
import ml_dtypes
from jax import lax
from jax.experimental import pallas as pl
from jax.experimental.pallas import tpu as pltpu

F32 = jnp.float32
BF16 = jnp.bfloat16

DEPTH = 2
FFN_RES = 0.5
POOL_WINDOWS = (2, 4, 8, 16)
POOL_HALO = 16
N_HEADS = 8
LN_EPS = 1e-5
RMS_EPS = 1e-5
DEEPNORM_ALPHA = (2.0 * DEPTH) ** 0.25

V7X_LANES = 128
V7X_VMEM_BYTES = 64 * 1024 * 1024
V7X_VMEM_RESERVE_BYTES = 8 * 1024 * 1024

POS_SPLIT_BITS = 6
MAX_SEQ = 1 << (POS_SPLIT_BITS + 8)


def _vmem_limit(*byte_counts):
    need = int(sum(byte_counts))
    return min(need + V7X_VMEM_RESERVE_BYTES, V7X_VMEM_BYTES - V7X_VMEM_RESERVE_BYTES)


def _nbytes(shape, dtype):
    return math.prod(shape) * jnp.dtype(dtype).itemsize


def _layer_norm(y, g, b):
    mu = jnp.mean(y, axis=-1, keepdims=True)
    yc = y - mu
    var = jnp.mean(yc * yc, axis=-1, keepdims=True)
    return yc * lax.rsqrt(var + LN_EPS) * g + b


def _const2(i):
    return (0, 0)


def _ffn_ln_body(x_ref, win_ref, wout_ref, g_ref, b_ref, o_ref, acc_ref, *, d_ff, chunk):
    x = x_ref[...]
    xb = x.astype(BF16)
    for c in range(d_ff // chunk):
        lo = c * chunk
        gate = jnp.dot(xb, win_ref[:, lo:lo + chunk], preferred_element_type=F32)
        up = jnp.dot(xb, win_ref[:, d_ff + lo:d_ff + lo + chunk], preferred_element_type=F32)
        act = (gate * jax.nn.sigmoid(gate) * up).astype(BF16)
        part = jnp.dot(act, wout_ref[lo:lo + chunk, :], preferred_element_type=F32)
        if c == 0:
            acc_ref[...] = part
        else:
            acc_ref[...] += part
    y = DEEPNORM_ALPHA * x + FFN_RES * acc_ref[...]
    o_ref[...] = _layer_norm(y, g_ref[...], b_ref[...])


def _ffn_ln(x, w_in, w_out, g, b, *, tm=512, chunk=256):
    n, d = x.shape
    d_ff = w_out.shape[0]
    assert n % tm == 0 and d_ff % chunk == 0 and w_in.shape == (d, 2 * d_ff)
    tile = _nbytes((tm, d), F32)
    vmem = _vmem_limit(_nbytes(w_in.shape, BF16), _nbytes(w_out.shape, BF16),
                       4 * tile,
                       tile,
                       3 * tile)
    return pl.pallas_call(
        functools.partial(_ffn_ln_body, d_ff=d_ff, chunk=chunk),
        out_shape=jax.ShapeDtypeStruct((n, d), F32),
        grid=(n // tm,),
        in_specs=[
            pl.BlockSpec((tm, d), lambda i: (i, 0)),
            pl.BlockSpec((d, 2 * d_ff), _const2, pipeline_mode=pl.Buffered(1)),
            pl.BlockSpec((d_ff, d), _const2, pipeline_mode=pl.Buffered(1)),
            pl.BlockSpec((1, d), _const2),
            pl.BlockSpec((1, d), _const2),
        ],
        out_specs=pl.BlockSpec((tm, d), lambda i: (i, 0)),
        scratch_shapes=[pltpu.VMEM((tm, d), F32)],
        compiler_params=pltpu.CompilerParams(dimension_semantics=("arbitrary",),
                                             vmem_limit_bytes=vmem),
        name="ffn_ln",
    )(x, w_in, w_out, g, b)


def _pool_ln_body(x_ref, halo_ref, w_ref, scale_ref, g_ref, b_ref, o_ref, *, seq_tiles):
    tm, d = x_ref.shape
    t = pl.program_id(0) % seq_tiles
    x = x_ref[...]
    ext = jnp.concatenate([jnp.where(t == 0, 0.0, halo_ref[...]), x], axis=0)
    pos = t * tm + lax.broadcasted_iota(jnp.int32, (tm, 1), 0)
    gw = d // len(POOL_WINDOWS)
    outs = []
    for gi, w in enumerate(POOL_WINDOWS):
        cols = slice(gi * gw, (gi + 1) * gw)
        run, span = ext[:, cols], 1
        while span < w:
            run = run + pltpu.roll(run, span, axis=0)
            span *= 2
        wsum = run[POOL_HALO:, :]
        count = jnp.minimum(pos + 1, w).astype(F32)
        dg = (wsum / count - x[:, cols]).astype(BF16)
        outs.append(jnp.dot(dg, w_ref[gi], preferred_element_type=F32))
    y = jnp.concatenate(outs, axis=-1) * scale_ref[...]
    o_ref[...] = _layer_norm(DEEPNORM_ALPHA * x + y, g_ref[...], b_ref[...])


def _pool_ln(x, w_pool, scale, g, b, *, seq, tm=512):
    n, d = x.shape
    assert seq % tm == 0 and tm % POOL_HALO == 0 and max(POOL_WINDOWS) <= POOL_HALO
    assert all(w & (w - 1) == 0 for w in POOL_WINDOWS)
    halo_blocks_per_tile = tm // POOL_HALO
    tile = _nbytes((tm, d), F32)
    vmem = _vmem_limit(2 * _nbytes(w_pool.shape, BF16), 4 * tile, tile, 4 * tile)
    return pl.pallas_call(
        functools.partial(_pool_ln_body, seq_tiles=seq // tm),
        out_shape=jax.ShapeDtypeStruct((n, d), F32),
        grid=(n // tm,),
        in_specs=[
            pl.BlockSpec((tm, d), lambda i: (i, 0)),
            pl.BlockSpec((POOL_HALO, d),
                         lambda i: (jnp.maximum(i * halo_blocks_per_tile - 1, 0), 0)),
            pl.BlockSpec(w_pool.shape, lambda i: (0, 0, 0)),
            pl.BlockSpec((1, d), _const2),
            pl.BlockSpec((1, d), _const2),
            pl.BlockSpec((1, d), _const2),
        ],
        out_specs=pl.BlockSpec((tm, d), lambda i: (i, 0)),
        compiler_params=pltpu.CompilerParams(dimension_semantics=("arbitrary",),
                                             vmem_limit_bytes=vmem),
        name="pool_ln",
    )(x, x, w_pool, scale, g, b)


_NT_DIMS = (((1,), (1,)), ((), ()))


def _bf16_pieces(value, n):
    pieces = []
    for _ in range(n):
        piece = float(ml_dtypes.bfloat16(value))
        pieces.append(piece)
        value -= piece
    return pieces


LOG2E_PIECES = _bf16_pieces(math.log2(math.e), 3)
N_BIAS = 2 * len(LOG2E_PIECES)
V_ROWS_PAD = 16


def _qkv_body(x_ref, wqT_ref, wk_ref, wvT_ref, q1T_ref, q2T_ref, k1_ref, k2_ref, vT_ref,
              *, head_dim, scale):
    t = pl.program_id(1)
    ts = x_ref.shape[0]
    width = 2 * head_dim
    n_heads = vT_ref.shape[2] // (width + V_ROWS_PAD)
    xb = x_ref[...].astype(BF16)
    qT = lax.dot_general(wqT_ref[...], xb, _NT_DIMS, preferred_element_type=F32)
    qT = qT * (scale * math.log2(math.e))
    vT = lax.dot_general(wvT_ref[...], xb, _NT_DIMS, preferred_element_type=F32)
    kk = jnp.dot(xb, wk_ref[...], preferred_element_type=F32)

    rr = lax.broadcasted_iota(jnp.int32, (head_dim, ts), 0)
    pair = rr >> 1
    log2e_rows = jnp.where(rr < N_BIAS,
                           jnp.where(pair == 0, LOG2E_PIECES[0],
                                     jnp.where(pair == 1, LOG2E_PIECES[1], LOG2E_PIECES[2])),
                           0.0).astype(BF16)
    ones_rows = (lax.broadcasted_iota(jnp.int32, (V_ROWS_PAD, ts), 0) == 0).astype(BF16)

    lane = lax.broadcasted_iota(jnp.int32, (1, width), 1)
    pos = t * ts + lax.broadcasted_iota(jnp.int32, (ts, 1), 0)
    hi = ((pos >> POS_SPLIT_BITS) << POS_SPLIT_BITS).astype(F32)
    lo = (pos & ((1 << POS_SPLIT_BITS) - 1)).astype(F32)
    hi_lo = 0.5 * jnp.where((lane & 1) == 0, hi, lo)
    bias1 = jnp.where((lane >= head_dim) & (lane < head_dim + N_BIAS), hi_lo, 0.0)
    bias2 = jnp.where(lane < N_BIAS, hi_lo, 0.0)
    first_half = lane < head_dim

    for h in range(n_heads):
        lo_r, mid_r, hi_r = h * width, h * width + head_dim, (h + 1) * width
        q1T_ref[0, 0, lo_r:mid_r, :] = qT[lo_r:mid_r, :].astype(BF16)
        q1T_ref[0, 0, mid_r:hi_r, :] = log2e_rows
        q2T_ref[0, 0, lo_r:mid_r, :] = log2e_rows
        q2T_ref[0, 0, mid_r:hi_r, :] = qT[mid_r:hi_r, :].astype(BF16)

        base = h * (width + V_ROWS_PAD)
        vT_ref[0, 0, base:base + width, :] = vT[lo_r:hi_r, :].astype(BF16)
        vT_ref[0, 0, base + width:base + width + V_ROWS_PAD, :] = ones_rows

        k_h = kk[:, lo_r:hi_r]
        head_scale = 2.0 ** -h
        k1_ref[0, :, lo_r:hi_r] = jnp.where(first_half, k_h, bias1 * head_scale).astype(BF16)
        k2_ref[0, :, lo_r:hi_r] = jnp.where(first_half, bias2 * head_scale, k_h).astype(BF16)


def _qkv_proj(x, wqT, wk, wvT, *, batch, seq, ts, head_dim, scale):
    n, d = x.shape
    hw = wk.shape[1]
    width = 2 * head_dim
    n_heads = hw // width
    st = seq // ts
    assert seq % ts == 0 and seq <= MAX_SEQ and N_BIAS <= head_dim
    tile = _nbytes((ts, hw), F32)
    vmem = _vmem_limit(2 * 3 * _nbytes(wk.shape, BF16), 2 * _nbytes((ts, d), F32),
                       2 * 6 * _nbytes((ts, hw), BF16), 8 * tile)
    row_out = jax.ShapeDtypeStruct((batch, seq, hw), BF16)
    col_out = jax.ShapeDtypeStruct((batch, st, hw, ts), BF16)
    v_rows = n_heads * (width + V_ROWS_PAD)
    v_out = jax.ShapeDtypeStruct((batch, st, v_rows, ts), BF16)
    row_spec = pl.BlockSpec((1, ts, hw), lambda b, t: (b, t, 0))
    col_spec = pl.BlockSpec((1, 1, hw, ts), lambda b, t: (b, t, 0, 0))
    v_spec = pl.BlockSpec((1, 1, v_rows, ts), lambda b, t: (b, t, 0, 0))
    return pl.pallas_call(
        functools.partial(_qkv_body, head_dim=head_dim, scale=scale),
        out_shape=(col_out, col_out, row_out, row_out, v_out),
        grid=(batch, st),
        in_specs=[
            pl.BlockSpec((ts, d), lambda b, t: (b * st + t, 0)),
            pl.BlockSpec(wqT.shape, lambda b, t: (0, 0)),
            pl.BlockSpec(wk.shape, lambda b, t: (0, 0)),
            pl.BlockSpec(wvT.shape, lambda b, t: (0, 0)),
        ],
        out_specs=(col_spec, col_spec, row_spec, row_spec, v_spec),
        compiler_params=pltpu.CompilerParams(dimension_semantics=("arbitrary", "arbitrary"),
                                             vmem_limit_bytes=vmem),
        name="qkv_proj",
    )(x, wqT, wk, wvT)


def _attn_body(q1T_ref, q2T_ref, k1_ref, k2_ref, vT_ref, lam_ref, g_ref, o_ref,
               acc1, acc2, m1, m2, s_a, s_b, mx_a, mx_b, *, lambda_init):
    ts = q1T_ref.shape[-1]
    tq = 2 * ts
    n_tiles = q1T_ref.shape[1] // 2
    dv = o_ref.shape[-1]
    units = tuple((mp, hf) for mp in (0, 1) for hf in (0, 1))

    lam = (jnp.exp(jnp.sum(lam_ref[0:1, :] * lam_ref[1:2, :], axis=-1, keepdims=True))
           - jnp.exp(jnp.sum(lam_ref[2:3, :] * lam_ref[3:4, :], axis=-1, keepdims=True))
           + lambda_init)

    def scores(i, j, s_buf, mx_buf, mp, hf):
        rows = pl.ds(pl.multiple_of(j * ts, ts), ts)
        k = (k1_ref, k2_ref)[mp][0, rows, :]
        qT = (q1T_ref, q2T_ref)[mp][0, 2 * i + hf]
        s = jnp.dot(k, qT, preferred_element_type=F32)
        s_buf[mp, :, hf * ts:(hf + 1) * ts] = s
        mx_buf[mp, :, hf * ts:(hf + 1) * ts] = jnp.max(s, axis=0, keepdims=True)

    def softmax_pv(j, s_buf, mx_buf, mp, hf, causal=False):
        acc, m = ((acc1, m1), (acc2, m2))[mp]
        cols = slice(hf * ts, (hf + 1) * ts)
        v = vT_ref[0, j]
        s = s_buf[mp, :, cols]
        if causal:
            key_idx = lax.broadcasted_iota(jnp.int32, s.shape, 0)
            qry_idx = lax.broadcasted_iota(jnp.int32, s.shape, 1)
            s = jnp.where(key_idx <= qry_idx, s, -jnp.inf)
            blk_max = jnp.max(s, axis=0, keepdims=True)
        else:
            blk_max = mx_buf[mp, :, cols]
        m_old = m[:, cols]
        m_new = jnp.maximum(m_old, blk_max)
        alpha = jnp.exp2(m_old - m_new)
        p = jnp.exp2(s - m_new).astype(BF16)
        acc[:, cols] = alpha * acc[:, cols] + jnp.dot(v, p, preferred_element_type=F32)
        m[:, cols] = m_new

    for mp, hf in units:
        scores(0, 0, s_a, mx_a, mp, hf)

    def query_tile(i, carry):
        for m, acc in ((m1, acc1), (m2, acc2)):
            m[...] = jnp.full(m.shape, -jnp.inf, F32)
            acc[...] = jnp.zeros(acc.shape, F32)

        def block_pair(j):
            for mp, hf in units:
                scores(i, j + 1, s_b, mx_b, mp, hf)
                softmax_pv(j, s_a, mx_a, mp, hf)
            for mp, hf in units:
                scores(i, j + 2, s_a, mx_a, mp, hf)
                softmax_pv(j + 1, s_b, mx_b, mp, hf)

        def two_pairs(t, inner):
            block_pair(4 * t)
            block_pair(4 * t + 2)
            return inner

        lax.fori_loop(0, lax.shift_right_logical(i, 1), two_pairs, 0)

        @pl.when((i & 1) == 1)
        def _():
            block_pair(2 * i - 2)

        for mp in (0, 1):
            scores(i, 2 * i + 1, s_b, mx_b, mp, 1)
            softmax_pv(2 * i, s_a, mx_a, mp, 0, causal=True)
            softmax_pv(2 * i, s_a, mx_a, mp, 1)
        for mp in (0, 1):
            softmax_pv(2 * i + 1, s_b, mx_b, mp, 1, causal=True)

        o1T = acc1[0:dv, :] / acc1[dv:dv + 1, :]
        o2T = acc2[0:dv, :] / acc2[dv:dv + 1, :]
        o = (o1T - lam * o2T).T
        o = o * lax.rsqrt(jnp.mean(o * o, axis=-1, keepdims=True) + RMS_EPS)
        o = o * g_ref[...] * (1.0 - lambda_init)
        o_ref[0, pl.ds(pl.multiple_of(i * tq, tq), tq), :] = o.astype(o_ref.dtype)
        nxt = jnp.minimum(i + 1, n_tiles - 1)
        for mp, hf in units:
            scores(nxt, 0, s_a, mx_a, mp, hf)
        return carry

    lax.fori_loop(0, n_tiles, query_tile, 0)


def _diff_attn(q1T, q2T, k1, k2, vT, lam_vecs, subln_g, *, lambda_init):
    batch, st, hw, ts = q1T.shape
    seq = st * ts
    width = hw // N_HEADS
    v_rows = width + V_ROWS_PAD
    tq = 2 * ts
    assert vT.shape == (batch, st, N_HEADS * v_rows, ts) and st % 2 == 0
    resident = 2 * (5 * _nbytes((seq, width), BF16) + _nbytes((st, v_rows, ts), BF16))
    score_buf = pltpu.VMEM((2, ts, tq), F32)
    max_buf = pltpu.VMEM((2, 1, tq), F32)
    acc_buf = pltpu.VMEM((v_rows, tq), F32)
    stat = pltpu.VMEM((1, tq), F32)
    vmem = _vmem_limit(resident,
                       2 * _nbytes((v_rows, tq), F32),
                       2 * _nbytes((2, ts, tq), F32),
                       4 * _nbytes((ts, tq), F32))
    q_spec = pl.BlockSpec((1, st, width, ts), lambda b, h: (b, 0, h, 0))
    k_spec = pl.BlockSpec((1, seq, width), lambda b, h: (b, 0, h))
    return pl.pallas_call(
        functools.partial(_attn_body, lambda_init=lambda_init),
        out_shape=jax.ShapeDtypeStruct((batch, seq, hw), BF16),
        grid=(batch, N_HEADS),
        in_specs=[
            q_spec, q_spec, k_spec, k_spec,
            pl.BlockSpec((1, st, v_rows, ts), lambda b, h: (b, 0, h, 0)),
            pl.BlockSpec(lam_vecs.shape, lambda b, h: (0, 0)),
            pl.BlockSpec((1, width), lambda b, h: (0, 0)),
        ],
        out_specs=pl.BlockSpec((1, seq, width), lambda b, h: (b, 0, h)),
        scratch_shapes=[acc_buf, acc_buf, stat, stat, score_buf, score_buf, max_buf, max_buf],
        compiler_params=pltpu.CompilerParams(
            dimension_semantics=("arbitrary", "arbitrary"), vmem_limit_bytes=vmem),
        name="diff_attn",
    )(q1T, q2T, k1, k2, vT, lam_vecs, subln_g)


def _proj_ln_body(x_ref, a_ref, w_ref, g_ref, b_ref, o_ref, *, sub):
    for r in range(x_ref.shape[0] // sub):
        rows = slice(r * sub, (r + 1) * sub)
        y = jnp.dot(a_ref[rows, :], w_ref[...], preferred_element_type=F32)
        o_ref[rows, :] = _layer_norm(DEEPNORM_ALPHA * x_ref[rows, :] + y, g_ref[...], b_ref[...])


def _proj_ln(x, a, w, g, b, *, tm=1024, sub=256):
    n, d = x.shape
    assert n % tm == 0 and tm % sub == 0
    tile = _nbytes((tm, d), F32)
    vmem = _vmem_limit(_nbytes(w.shape, BF16), 4 * tile, tile, 2 * tile)
    return pl.pallas_call(
        functools.partial(_proj_ln_body, sub=sub),
        out_shape=jax.ShapeDtypeStruct((n, d), F32),
        grid=(n // tm,),
        in_specs=[
            pl.BlockSpec((tm, d), lambda i: (i, 0)),
            pl.BlockSpec((tm, a.shape[1]), lambda i: (i, 0)),
            pl.BlockSpec(w.shape, _const2, pipeline_mode=pl.Buffered(1)),
            pl.BlockSpec((1, d), _const2),
            pl.BlockSpec((1, d), _const2),
        ],
        out_specs=pl.BlockSpec((tm, d), lambda i: (i, 0)),
        compiler_params=pltpu.CompilerParams(dimension_semantics=("arbitrary",),
                                             vmem_limit_bytes=vmem),
        name="proj_ln",
    )(x, a, w, g, b)


def _row(v):
    return v.reshape(1, -1).astype(F32)


def _ffn(h, w_in, w_out, g, b):
    return _ffn_ln(h, w_in.astype(BF16), w_out.astype(BF16), _row(g), _row(b))


def kernel(x, l0_ffn1_w_in, l0_ffn1_w_out, l0_ln1_g, l0_ln1_b, l0_pool_w, l0_pool_scale, l0_ln2_g, l0_ln2_b, l0_ffn2_w_in, l0_ffn2_w_out, l0_ln3_g, l0_ln3_b, l1_ffn1_w_in, l1_ffn1_w_out, l1_ln1_g, l1_ln1_b, l1_w_qkv, l1_lam_q1, l1_lam_k1, l1_lam_q2, l1_lam_k2, l1_subln_g, l1_w_o, l1_ln2_g, l1_ln2_b, l1_ffn2_w_in, l1_ffn2_w_out, l1_ln3_g, l1_ln3_b):
    batch, seq, d = x.shape
    h = x.reshape(batch * seq, d)

    h = _ffn(h, l0_ffn1_w_in, l0_ffn1_w_out, l0_ln1_g, l0_ln1_b)
    h = _pool_ln(h, l0_pool_w.astype(BF16), _row(l0_pool_scale), _row(l0_ln2_g), _row(l0_ln2_b),
                 seq=seq)
    h = _ffn(h, l0_ffn2_w_in, l0_ffn2_w_out, l0_ln3_g, l0_ln3_b)

    h = _ffn(h, l1_ffn1_w_in, l1_ffn1_w_out, l1_ln1_g, l1_ln1_b)

    head_dim = d // (2 * N_HEADS)
    qk_w = N_HEADS * head_dim
    assert 2 * head_dim == V7X_LANES and l1_w_qkv.shape == (d, 4 * qk_w + N_HEADS * 2 * head_dim)

    def per_head_pairs(w_a, w_b):
        pair = jnp.stack([w_a.reshape(d, N_HEADS, head_dim), w_b.reshape(d, N_HEADS, head_dim)], 2)
        return pair.reshape(d, N_HEADS * 2 * head_dim)

    w = l1_w_qkv.astype(BF16)
    wqT = per_head_pairs(w[:, :qk_w], w[:, qk_w:2 * qk_w]).T
    wk = per_head_pairs(w[:, 2 * qk_w:3 * qk_w], w[:, 3 * qk_w:4 * qk_w])
    wvT = w[:, 4 * qk_w:].T
    q1T, q2T, k1, k2, vT = _qkv_proj(h, wqT, wk, wvT, batch=batch, seq=seq, ts=512,
                                     head_dim=head_dim, scale=head_dim ** -0.5)
    lambda_init = 0.8 - 0.6 * math.exp(-0.3 * 1)
    lam_vecs = jnp.stack([l1_lam_q1, l1_lam_k1, l1_lam_q2, l1_lam_k2]).astype(F32)
    attn = _diff_attn(q1T, q2T, k1, k2, vT, lam_vecs, _row(l1_subln_g), lambda_init=lambda_init)
    h = _proj_ln(h, attn.reshape(batch * seq, d), l1_w_o.astype(BF16), _row(l1_ln2_g),
                 _row(l1_ln2_b))

    h = _ffn(h, l1_ffn2_w_in, l1_ffn2_w_out, l1_ln3_g, l1_ln3_b)
    return h.reshape(batch, seq, d)
```

```python
import functools
import math

import jax
import jax.numpy as jnp
import ml_dtypes
from jax import lax
from jax.experimental import pallas as pl
from jax.experimental.pallas import tpu as pltpu

F32 = jnp.float32
BF16 = jnp.bfloat16

DEPTH = 2
FFN_RES = 0.5
POOL_WINDOWS = (2, 4, 8, 16)
POOL_HALO = 16
N_HEADS = 8
LN_EPS = 1e-5
RMS_EPS = 1e-5
DEEPNORM_ALPHA = (2.0 * DEPTH) ** 0.25

V7X_LANES = 128
V7X_VMEM_BYTES = 64 * 1024 * 1024
V7X_VMEM_RESERVE_BYTES = 8 * 1024 * 1024

POS_SPLIT_BITS = 6
MAX_SEQ = 1 << (POS_SPLIT_BITS + 8)


def _vmem_limit(*byte_counts):
    need = int(sum(byte_counts))
    return min(need + V7X_VMEM_RESERVE_BYTES, V7X_VMEM_BYTES - V7X_VMEM_RESERVE_BYTES)


def _nbytes(shape, dtype):
    return math.prod(shape) * jnp.dtype(dtype).itemsize


def _layer_norm(y, g, b):
    mu = jnp.mean(y, axis=-1, keepdims=True)
    yc = y - mu
    var = jnp.mean(yc * yc, axis=-1, keepdims=True)
    return yc * lax.rsqrt(var + LN_EPS) * g + b


def _const2(i):
    return (0, 0)


def _zero_after(value):
    bits = pltpu.bitcast(value[0:8, 0:V7X_LANES], jnp.uint32)
    zero = lax.shift_right_logical(lax.shift_right_logical(bits, jnp.uint32(16)), jnp.uint32(16))
    return zero[0:1, 0:1].astype(F32)


def _pool_windows(h, prev, pos0):
    rows, d = h.shape
    ext = jnp.concatenate([prev, h], axis=0)
    pos = pos0 + lax.broadcasted_iota(jnp.int32, (rows, 1), 0)
    gw = d // len(POOL_WINDOWS)
    dgs = []
    for gi, w in enumerate(POOL_WINDOWS):
        cols = slice(gi * gw, (gi + 1) * gw)
        run, span = ext[:, cols], 1
        while span < w:
            run = run + pltpu.roll(run, span, axis=0)
            span *= 2
        wsum = run[POOL_HALO:, :]
        count = jnp.minimum(pos + 1, w).astype(F32)
        dgs.append((wsum / count - h[:, cols]).astype(BF16))
    return dgs


def _pool_project(dgs, w_ref, scale):
    outs = [jnp.dot(dg, w_ref[gi], preferred_element_type=F32) for gi, dg in enumerate(dgs)]
    return jnp.concatenate(outs, axis=-1) * scale


def _ffn_body(*refs, d_ff, chunk, lag, ln_rows, seq_tiles):
    if seq_tiles is None:
        x_ref, win_ref, wout_ref, g1_ref, b1_ref, o_ref, acc_ref, y_ref = refs
    else:
        (x_ref, win_ref, wout_ref, g1_ref, b1_ref, wp_ref, sp_ref, g2_ref, b2_ref,
         o_ref, acc_ref, y_ref, halo_ref) = refs
    tm = x_ref.shape[0]
    n_chunks = d_ff // chunk
    n_groups = tm // ln_rows
    step = pl.program_id(0)

    @pl.when(step == 0)
    def _():
        y_ref[...] = jnp.zeros(y_ref.shape, F32)
        if seq_tiles is not None:
            halo_ref[...] = jnp.zeros(halo_ref.shape, F32)

    x = x_ref[...]
    xb = x.astype(BF16)
    acts = {}
    pending = []
    prev_rows = [None]

    def start_group(k, anchor):
        rows = slice(k * ln_rows, (k + 1) * ln_rows)
        h = _layer_norm(y_ref[rows, :] + _zero_after(anchor), g1_ref[...], b1_ref[...])
        if seq_tiles is None:
            o_ref[rows, :] = h
            return
        t_prev = (step + (seq_tiles - 1)) % seq_tiles
        if k == 0:
            prev = jnp.where(t_prev == 0, 0.0, halo_ref[...])
        else:
            prev = prev_rows[0]
        prev_rows[0] = h[ln_rows - POOL_HALO:, :]
        if k == n_groups - 1:
            halo_ref[...] = prev_rows[0]
        pending.append((rows, h, _pool_windows(h, prev, t_prev * tm + k * ln_rows)))

    def finish_group():
        if pending:
            rows, h, dgs = pending.pop()
            mix = _pool_project(dgs, wp_ref, sp_ref[...])
            o_ref[rows, :] = _layer_norm(DEEPNORM_ALPHA * h + mix, g2_ref[...], b2_ref[...])

    def down(c):
        lo = c * chunk
        part = jnp.dot(acts.pop(c), wout_ref[lo:lo + chunk, :], preferred_element_type=F32)
        if c == 0:
            acc_ref[...] = part
        else:
            acc_ref[...] += part
        if c < n_groups:
            start_group(c, part)

    for c in range(n_chunks):
        lo = c * chunk
        gate = jnp.dot(xb, win_ref[:, lo:lo + chunk], preferred_element_type=F32)
        up = jnp.dot(xb, win_ref[:, d_ff + lo:d_ff + lo + chunk], preferred_element_type=F32)
        acts[c] = (gate * jax.nn.sigmoid(gate) * up).astype(BF16)
        finish_group()
        if c >= lag:
            down(c - lag)
    for c in range(max(n_chunks - lag, 0), n_chunks):
        finish_group()
        down(c)
    finish_group()
    y_ref[...] = DEEPNORM_ALPHA * x + FFN_RES * acc_ref[...]


def _ffn(x, w_in, w_out, g1, b1, pool=None, *, seq=None, tm=512, chunk=256, lag=2, ln_rows=64):
    n, d = x.shape
    d_ff = w_out.shape[0]
    n_tiles = n // tm
    assert n % tm == 0 and d_ff % chunk == 0 and w_in.shape == (d, 2 * d_ff)
    assert tm % ln_rows == 0 and tm // ln_rows <= d_ff // chunk and ln_rows >= POOL_HALO
    tile = _nbytes((tm, d), F32)
    row = pl.BlockSpec((1, d), _const2)
    in_specs = [
        pl.BlockSpec((tm, d), lambda i: (jnp.minimum(i, n_tiles - 1), 0)),
        pl.BlockSpec((d, 2 * d_ff), _const2, pipeline_mode=pl.Buffered(1)),
        pl.BlockSpec((d_ff, d), _const2, pipeline_mode=pl.Buffered(1)),
        row, row,
    ]
    args = [x, w_in, w_out, g1, b1]
    scratch = [pltpu.VMEM((tm, d), F32), pltpu.VMEM((tm, d), F32)]
    seq_tiles = None
    pool_bytes = 0
    if pool is not None:
        w_pool, scale, g2, b2 = pool
        assert seq % tm == 0 and max(POOL_WINDOWS) <= POOL_HALO
        assert all(w & (w - 1) == 0 for w in POOL_WINDOWS)
        seq_tiles = seq // tm
        in_specs += [pl.BlockSpec(w_pool.shape, lambda i: (0, 0, 0)), row, row, row]
        args += [w_pool, scale, g2, b2]
        scratch.append(pltpu.VMEM((POOL_HALO, d), F32))
        pool_bytes = 2 * _nbytes(w_pool.shape, BF16)
    vmem = _vmem_limit(_nbytes(w_in.shape, BF16), _nbytes(w_out.shape, BF16), pool_bytes,
                       4 * tile,
                       2 * tile,
                       3 * tile)
    return pl.pallas_call(
        functools.partial(_ffn_body, d_ff=d_ff, chunk=chunk, lag=lag, ln_rows=ln_rows,
                          seq_tiles=seq_tiles),
        out_shape=jax.ShapeDtypeStruct((n, d), F32),
        grid=(n_tiles + 1,),
        in_specs=in_specs,
        out_specs=pl.BlockSpec((tm, d), lambda i: (jnp.maximum(i - 1, 0), 0)),
        scratch_shapes=scratch,
        compiler_params=pltpu.CompilerParams(dimension_semantics=("arbitrary",),
                                             vmem_limit_bytes=vmem),
        name="ffn_ln" if pool is None else "ffn_ln_pool_ln",
    )(*args)


_NT_DIMS = (((1,), (1,)), ((), ()))


def _bf16_pieces(value, n):
    pieces = []
    for _ in range(n):
        piece = float(ml_dtypes.bfloat16(value))
        pieces.append(piece)
        value -= piece
    return pieces


LOG2E_PIECES = _bf16_pieces(math.log2(math.e), 3)
N_BIAS = 2 * len(LOG2E_PIECES)
V_ROWS_PAD = 16


def _qkv_body(x_ref, wqT_ref, wk_ref, wvT_ref, q1T_ref, q2T_ref, k1_ref, k2_ref, vT_ref,
              *, head_dim, scale):
    t = pl.program_id(1)
    ts = x_ref.shape[0]
    width = 2 * head_dim
    n_heads = vT_ref.shape[2] // (width + V_ROWS_PAD)
    xb = x_ref[...].astype(BF16)
    qT = lax.dot_general(wqT_ref[...], xb, _NT_DIMS, preferred_element_type=F32)
    qT = qT * (scale * math.log2(math.e))
    vT = lax.dot_general(wvT_ref[...], xb, _NT_DIMS, preferred_element_type=F32)
    kk = jnp.dot(xb, wk_ref[...], preferred_element_type=F32)

    rr = lax.broadcasted_iota(jnp.int32, (head_dim, ts), 0)
    pair = rr >> 1
    log2e_rows = jnp.where(rr < N_BIAS,
                           jnp.where(pair == 0, LOG2E_PIECES[0],
                                     jnp.where(pair == 1, LOG2E_PIECES[1], LOG2E_PIECES[2])),
                           0.0).astype(BF16)
    ones_rows = (lax.broadcasted_iota(jnp.int32, (V_ROWS_PAD, ts), 0) == 0).astype(BF16)

    lane = lax.broadcasted_iota(jnp.int32, (1, width), 1)
    pos = t * ts + lax.broadcasted_iota(jnp.int32, (ts, 1), 0)
    hi = ((pos >> POS_SPLIT_BITS) << POS_SPLIT_BITS).astype(F32)
    lo = (pos & ((1 << POS_SPLIT_BITS) - 1)).astype(F32)
    hi_lo = 0.5 * jnp.where((lane & 1) == 0, hi, lo)
    bias1 = jnp.where((lane >= head_dim) & (lane < head_dim + N_BIAS), hi_lo, 0.0)
    bias2 = jnp.where(lane < N_BIAS, hi_lo, 0.0)
    first_half = lane < head_dim

    for h in range(n_heads):
        lo_r, mid_r, hi_r = h * width, h * width + head_dim, (h + 1) * width
        q1T_ref[0, 0, lo_r:mid_r, :] = qT[lo_r:mid_r, :].astype(BF16)
        q1T_ref[0, 0, mid_r:hi_r, :] = log2e_rows
        q2T_ref[0, 0, lo_r:mid_r, :] = log2e_rows
        q2T_ref[0, 0, mid_r:hi_r, :] = qT[mid_r:hi_r, :].astype(BF16)

        base = h * (width + V_ROWS_PAD)
        vT_ref[0, 0, base:base + width, :] = vT[lo_r:hi_r, :].astype(BF16)
        vT_ref[0, 0, base + width:base + width + V_ROWS_PAD, :] = ones_rows

        k_h = kk[:, lo_r:hi_r]
        head_scale = 2.0 ** -h
        k1_ref[0, :, lo_r:hi_r] = jnp.where(first_half, k_h, bias1 * head_scale).astype(BF16)
        k2_ref[0, :, lo_r:hi_r] = jnp.where(first_half, bias2 * head_scale, k_h).astype(BF16)


def _qkv_proj(x, wqT, wk, wvT, *, batch, seq, ts, head_dim, scale):
    n, d = x.shape
    hw = wk.shape[1]
    width = 2 * head_dim
    n_heads = hw // width
    st = seq // ts
    assert seq % ts == 0 and seq <= MAX_SEQ and N_BIAS <= head_dim
    tile = _nbytes((ts, hw), F32)
    vmem = _vmem_limit(2 * 3 * _nbytes(wk.shape, BF16), 2 * _nbytes((ts, d), F32),
                       2 * 6 * _nbytes((ts, hw), BF16), 8 * tile)
    row_out = jax.ShapeDtypeStruct((batch, seq, hw), BF16)
    col_out = jax.ShapeDtypeStruct((batch, st, hw, ts), BF16)
    v_rows = n_heads * (width + V_ROWS_PAD)
    v_out = jax.ShapeDtypeStruct((batch, st, v_rows, ts), BF16)
    row_spec = pl.BlockSpec((1, ts, hw), lambda b, t: (b, t, 0))
    col_spec = pl.BlockSpec((1, 1, hw, ts), lambda b, t: (b, t, 0, 0))
    v_spec = pl.BlockSpec((1, 1, v_rows, ts), lambda b, t: (b, t, 0, 0))
    return pl.pallas_call(
        functools.partial(_qkv_body, head_dim=head_dim, scale=scale),
        out_shape=(col_out, col_out, row_out, row_out, v_out),
        grid=(batch, st),
        in_specs=[
            pl.BlockSpec((ts, d), lambda b, t: (b * st + t, 0)),
            pl.BlockSpec(wqT.shape, lambda b, t: (0, 0)),
            pl.BlockSpec(wk.shape, lambda b, t: (0, 0)),
            pl.BlockSpec(wvT.shape, lambda b, t: (0, 0)),
        ],
        out_specs=(col_spec, col_spec, row_spec, row_spec, v_spec),
        compiler_params=pltpu.CompilerParams(dimension_semantics=("arbitrary", "arbitrary"),
                                             vmem_limit_bytes=vmem),
        name="qkv_proj",
    )(x, wqT, wk, wvT)


def _attn_body(q1T_ref, q2T_ref, k1_ref, k2_ref, vT_ref, lam_ref, g_ref, o_ref,
               acc1, acc2, m1, m2, s_a, s_b, mx_a, mx_b, *, lambda_init):
    ts = q1T_ref.shape[-1]
    tq = 2 * ts
    n_tiles = q1T_ref.shape[1] // 2
    dv = o_ref.shape[-1]
    units = tuple((mp, hf) for mp in (0, 1) for hf in (0, 1))

    lam = (jnp.exp(jnp.sum(lam_ref[0:1, :] * lam_ref[1:2, :], axis=-1, keepdims=True))
           - jnp.exp(jnp.sum(lam_ref[2:3, :] * lam_ref[3:4, :], axis=-1, keepdims=True))
           + lambda_init)

    def scores(i, j, s_buf, mx_buf, mp, hf):
        rows = pl.ds(pl.multiple_of(j * ts, ts), ts)
        k = (k1_ref, k2_ref)[mp][0, rows, :]
        qT = (q1T_ref, q2T_ref)[mp][0, 2 * i + hf]
        s = jnp.dot(k, qT, preferred_element_type=F32)
        s_buf[mp, :, hf * ts:(hf + 1) * ts] = s
        mx_buf[mp, :, hf * ts:(hf + 1) * ts] = jnp.max(s, axis=0, keepdims=True)

    def softmax_pv(j, s_buf, mx_buf, mp, hf, causal=False):
        acc, m = ((acc1, m1), (acc2, m2))[mp]
        cols = slice(hf * ts, (hf + 1) * ts)
        v = vT_ref[0, j]
        s = s_buf[mp, :, cols]
        if causal:
            key_idx = lax.broadcasted_iota(jnp.int32, s.shape, 0)
            qry_idx = lax.broadcasted_iota(jnp.int32, s.shape, 1)
            s = jnp.where(key_idx <= qry_idx, s, -jnp.inf)
            blk_max = jnp.max(s, axis=0, keepdims=True)
        else:
            blk_max = mx_buf[mp, :, cols]
        m_old = m[:, cols]
        m_new = jnp.maximum(m_old, blk_max)
        alpha = jnp.exp2(m_old - m_new)
        p = jnp.exp2(s - m_new).astype(BF16)
        acc[:, cols] = alpha * acc[:, cols] + jnp.dot(v, p, preferred_element_type=F32)
        m[:, cols] = m_new

    for mp, hf in units:
        scores(0, 0, s_a, mx_a, mp, hf)

    def query_tile(i, carry):
        for m, acc in ((m1, acc1), (m2, acc2)):
            m[...] = jnp.full(m.shape, -jnp.inf, F32)
            acc[...] = jnp.zeros(acc.shape, F32)

        def block_pair(j):
            for mp, hf in units:
                scores(i, j + 1, s_b, mx_b, mp, hf)
                softmax_pv(j, s_a, mx_a, mp, hf)
            for mp, hf in units:
                scores(i, j + 2, s_a, mx_a, mp, hf)
                softmax_pv(j + 1, s_b, mx_b, mp, hf)

        def two_pairs(t, inner):
            block_pair(4 * t)
            block_pair(4 * t + 2)
            return inner

        lax.fori_loop(0, lax.shift_right_logical(i, 1), two_pairs, 0)

        @pl.when((i & 1) == 1)
        def _():
            block_pair(2 * i - 2)

        for mp in (0, 1):
            scores(i, 2 * i + 1, s_b, mx_b, mp, 1)
            softmax_pv(2 * i, s_a, mx_a, mp, 0, causal=True)
            softmax_pv(2 * i, s_a, mx_a, mp, 1)
        for mp in (0, 1):
            softmax_pv(2 * i + 1, s_b, mx_b, mp, 1, causal=True)

        o1T = acc1[0:dv, :] / acc1[dv:dv + 1, :]
        o2T = acc2[0:dv, :] / acc2[dv:dv + 1, :]
        o = (o1T - lam * o2T).T
        o = o * lax.rsqrt(jnp.mean(o * o, axis=-1, keepdims=True) + RMS_EPS)
        o = o * g_ref[...] * (1.0 - lambda_init)
        o_ref[0, pl.ds(pl.multiple_of(i * tq, tq), tq), :] = o.astype(o_ref.dtype)
        nxt = jnp.minimum(i + 1, n_tiles - 1)
        for mp, hf in units:
            scores(nxt, 0, s_a, mx_a, mp, hf)
        return carry

    lax.fori_loop(0, n_tiles, query_tile, 0)


def _diff_attn(q1T, q2T, k1, k2, vT, lam_vecs, subln_g, *, lambda_init):
    batch, st, hw, ts = q1T.shape
    seq = st * ts
    width = hw // N_HEADS
    v_rows = width + V_ROWS_PAD
    tq = 2 * ts
    assert vT.shape == (batch, st, N_HEADS * v_rows, ts) and st % 2 == 0
    resident = 2 * (5 * _nbytes((seq, width), BF16) + _nbytes((st, v_rows, ts), BF16))
    score_buf = pltpu.VMEM((2, ts, tq), F32)
    max_buf = pltpu.VMEM((2, 1, tq), F32)
    acc_buf = pltpu.VMEM((v_rows, tq), F32)
    stat = pltpu.VMEM((1, tq), F32)
    vmem = _vmem_limit(resident,
                       2 * _nbytes((v_rows, tq), F32),
                       2 * _nbytes((2, ts, tq), F32),
                       4 * _nbytes((ts, tq), F32))
    q_spec = pl.BlockSpec((1, st, width, ts), lambda b, h: (b, 0, h, 0))
    k_spec = pl.BlockSpec((1, seq, width), lambda b, h: (b, 0, h))
    return pl.pallas_call(
        functools.partial(_attn_body, lambda_init=lambda_init),
        out_shape=jax.ShapeDtypeStruct((batch, seq, hw), BF16),
        grid=(batch, N_HEADS),
        in_specs=[
            q_spec, q_spec, k_spec, k_spec,
            pl.BlockSpec((1, st, v_rows, ts), lambda b, h: (b, 0, h, 0)),
            pl.BlockSpec(lam_vecs.shape, lambda b, h: (0, 0)),
            pl.BlockSpec((1, width), lambda b, h: (0, 0)),
        ],
        out_specs=pl.BlockSpec((1, seq, width), lambda b, h: (b, 0, h)),
        scratch_shapes=[acc_buf, acc_buf, stat, stat, score_buf, score_buf, max_buf, max_buf],
        compiler_params=pltpu.CompilerParams(
            dimension_semantics=("arbitrary", "arbitrary"), vmem_limit_bytes=vmem),
        name="diff_attn",
    )(q1T, q2T, k1, k2, vT, lam_vecs, subln_g)


def _proj_ln_body(x_ref, a_ref, w_ref, g_ref, b_ref, o_ref, *, sub):
    for r in range(x_ref.shape[0] // sub):
        rows = slice(r * sub, (r + 1) * sub)
        y = jnp.dot(a_ref[rows, :], w_ref[...], preferred_element_type=F32)
        o_ref[rows, :] = _layer_norm(DEEPNORM_ALPHA * x_ref[rows, :] + y, g_ref[...], b_ref[...])


def _proj_ln(x, a, w, g, b, *, tm=1024, sub=256):
    n, d = x.shape
    assert n % tm == 0 and tm % sub == 0
    tile = _nbytes((tm, d), F32)
    vmem = _vmem_limit(_nbytes(w.shape, BF16), 4 * tile, tile, 2 * tile)
    return pl.pallas_call(
        functools.partial(_proj_ln_body, sub=sub),
        out_shape=jax.ShapeDtypeStruct((n, d), F32),
        grid=(n // tm,),
        in_specs=[
            pl.BlockSpec((tm, d), lambda i: (i, 0)),
            pl.BlockSpec((tm, a.shape[1]), lambda i: (i, 0)),
            pl.BlockSpec(w.shape, _const2, pipeline_mode=pl.Buffered(1)),
            pl.BlockSpec((1, d), _const2),
            pl.BlockSpec((1, d), _const2),
        ],
        out_specs=pl.BlockSpec((tm, d), lambda i: (i, 0)),
        compiler_params=pltpu.CompilerParams(dimension_semantics=("arbitrary",),
                                             vmem_limit_bytes=vmem),
        name="proj_ln",
    )(x, a, w, g, b)


def _row(v):
    return v.reshape(1, -1).astype(F32)


def _cast_ffn(w_in, w_out, g, b):
    return w_in.astype(BF16), w_out.astype(BF16), _row(g), _row(b)


def kernel(x, l0_ffn1_w_in, l0_ffn1_w_out, l0_ln1_g, l0_ln1_b, l0_pool_w, l0_pool_scale, l0_ln2_g, l0_ln2_b, l0_ffn2_w_in, l0_ffn2_w_out, l0_ln3_g, l0_ln3_b, l1_ffn1_w_in, l1_ffn1_w_out, l1_ln1_g, l1_ln1_b, l1_w_qkv, l1_lam_q1, l1_lam_k1, l1_lam_q2, l1_lam_k2, l1_subln_g, l1_w_o, l1_ln2_g, l1_ln2_b, l1_ffn2_w_in, l1_ffn2_w_out, l1_ln3_g, l1_ln3_b):
    batch, seq, d = x.shape
    h = x.reshape(batch * seq, d)

    pool = (l0_pool_w.astype(BF16), _row(l0_pool_scale), _row(l0_ln2_g), _row(l0_ln2_b))
    h = _ffn(h, *_cast_ffn(l0_ffn1_w_in, l0_ffn1_w_out, l0_ln1_g, l0_ln1_b), pool, seq=seq)
    h = _ffn(h, *_cast_ffn(l0_ffn2_w_in, l0_ffn2_w_out, l0_ln3_g, l0_ln3_b))

    h = _ffn(h, *_cast_ffn(l1_ffn1_w_in, l1_ffn1_w_out, l1_ln1_g, l1_ln1_b))

    head_dim = d // (2 * N_HEADS)
    qk_w = N_HEADS * head_dim
    assert 2 * head_dim == V7X_LANES and l1_w_qkv.shape == (d, 4 * qk_w + N_HEADS * 2 * head_dim)

    def per_head_pairs(w_a, w_b):
        pair = jnp.stack([w_a.reshape(d, N_HEADS, head_dim), w_b.reshape(d, N_HEADS, head_dim)], 2)
        return pair.reshape(d, N_HEADS * 2 * head_dim)

    w = l1_w_qkv.astype(BF16)
    wqT = per_head_pairs(w[:, :qk_w], w[:, qk_w:2 * qk_w]).T
    wk = per_head_pairs(w[:, 2 * qk_w:3 * qk_w], w[:, 3 * qk_w:4 * qk_w])
    wvT = w[:, 4 * qk_w:].T
    q1T, q2T, k1, k2, vT = _qkv_proj(h, wqT, wk, wvT, batch=batch, seq=seq, ts=512,
                                     head_dim=head_dim, scale=head_dim ** -0.5)
    lambda_init = 0.8 - 0.6 * math.exp(-0.3 * 1)
    lam_vecs = jnp.stack([l1_lam_q1, l1_lam_k1, l1_lam_q2, l1_lam_k2]).astype(F32)
    attn = _diff_attn(q1T, q2T, k1, k2, vT, lam_vecs, _row(l1_subln_g), lambda_init=lambda_init)
    h = _proj_ln(h, attn.reshape(batch * seq, d), l1_w_o.astype(BF16), _row(l1_ln2_g),
                 _row(l1_ln2_b))

    h = _ffn(h, *_cast_ffn(l1_ffn2_w_in, l1_ffn2_w_out, l1_ln3_g, l1_ln3_b))
    return h.reshape(batch, seq, d)
```

```python
import functools
import math

import jax
import jax.numpy as jnp
import ml_dtypes
from jax import lax
from jax.experimental import pallas as pl
from jax.experimental.pallas import tpu as pltpu

F32 = jnp.float32
BF16 = jnp.bfloat16

DEPTH = 2
FFN_RES = 0.5
POOL_WINDOWS = (2, 4, 8, 16)
POOL_HALO = 16
N_HEADS = 8
LN_EPS = 1e-5
RMS_EPS = 1e-5
DEEPNORM_ALPHA = (2.0 * DEPTH) ** 0.25

V7X_LANES = 128
V7X_VMEM_BYTES = 64 * 1024 * 1024
V7X_VMEM_RESERVE_BYTES = 8 * 1024 * 1024

POS_SPLIT_BITS = 6
MAX_SEQ = 1 << (POS_SPLIT_BITS + 8)


def _vmem_limit(*byte_counts):
    need = int(sum(byte_counts))
    return min(need + V7X_VMEM_RESERVE_BYTES, V7X_VMEM_BYTES - V7X_VMEM_RESERVE_BYTES)


def _nbytes(shape, dtype):
    return math.prod(shape) * jnp.dtype(dtype).itemsize


def _layer_norm(y, g, b):
    mu = jnp.mean(y, axis=-1, keepdims=True)
    yc = y - mu
    var = jnp.mean(yc * yc, axis=-1, keepdims=True)
    return yc * lax.rsqrt(var + LN_EPS) * g + b


def _const2(i):
    return (0, 0)


def _zero_after(value):
    bits = pltpu.bitcast(value[0:8, 0:V7X_LANES], jnp.uint32)
    zero = lax.shift_right_logical(lax.shift_right_logical(bits, jnp.uint32(16)), jnp.uint32(16))
    return zero[0:1, 0:1].astype(F32)


def _pool_windows(h, prev, pos0):
    rows, d = h.shape
    ext = jnp.concatenate([prev, h], axis=0)
    pos = pos0 + lax.broadcasted_iota(jnp.int32, (rows, 1), 0)
    gw = d // len(POOL_WINDOWS)
    dgs = []
    for gi, w in enumerate(POOL_WINDOWS):
        cols = slice(gi * gw, (gi + 1) * gw)
        run, span = ext[:, cols], 1
        while span < w:
            run = run + pltpu.roll(run, span, axis=0)
            span *= 2
        wsum = run[POOL_HALO:, :]
        count = jnp.minimum(pos + 1, w).astype(F32)
        dgs.append((wsum / count - h[:, cols]).astype(BF16))
    return dgs


def _pool_project(dgs, w_ref, scale):
    outs = [jnp.dot(dg, w_ref[gi], preferred_element_type=F32) for gi, dg in enumerate(dgs)]
    return jnp.concatenate(outs, axis=-1) * scale


def _ffn_body(*refs, d_ff, chunk, lag, ln_rows, seq_tiles):
    if seq_tiles is None:
        x_ref, win_ref, wout_ref, g1_ref, b1_ref, o_ref, acc_ref, y_ref = refs
    else:
        (x_ref, win_ref, wout_ref, g1_ref, b1_ref, wp_ref, sp_ref, g2_ref, b2_ref,
         o_ref, acc_ref, y_ref, halo_ref) = refs
    tm = x_ref.shape[0]
    n_chunks = d_ff // chunk
    n_groups = tm // ln_rows
    step = pl.program_id(0)

    @pl.when(step == 0)
    def _():
        y_ref[...] = jnp.zeros(y_ref.shape, F32)
        if seq_tiles is not None:
            halo_ref[...] = jnp.zeros(halo_ref.shape, F32)

    x = x_ref[...]
    xb = x.astype(BF16)
    acts = {}
    pending = []
    prev_rows = [None]

    def start_group(k, anchor):
        rows = slice(k * ln_rows, (k + 1) * ln_rows)
        h = _layer_norm(y_ref[rows, :] + _zero_after(anchor), g1_ref[...], b1_ref[...])
        if seq_tiles is None:
            o_ref[rows, :] = h
            return
        t_prev = (step + (seq_tiles - 1)) % seq_tiles
        if k == 0:
            prev = jnp.where(t_prev == 0, 0.0, halo_ref[...])
        else:
            prev = prev_rows[0]
        prev_rows[0] = h[ln_rows - POOL_HALO:, :]
        if k == n_groups - 1:
            halo_ref[...] = prev_rows[0]
        pending.append((rows, h, _pool_windows(h, prev, t_prev * tm + k * ln_rows)))

    def finish_group():
        if pending:
            rows, h, dgs = pending.pop()
            mix = _pool_project(dgs, wp_ref, sp_ref[...])
            o_ref[rows, :] = _layer_norm(DEEPNORM_ALPHA * h + mix, g2_ref[...], b2_ref[...])

    def down(c):
        lo = c * chunk
        part = jnp.dot(acts.pop(c), wout_ref[lo:lo + chunk, :], preferred_element_type=F32)
        if c == 0:
            acc_ref[...] = part
        else:
            acc_ref[...] += part
        if c < n_groups:
            start_group(c, part)

    for c in range(n_chunks):
        lo = c * chunk
        gate = jnp.dot(xb, win_ref[:, lo:lo + chunk], preferred_element_type=F32)
        up = jnp.dot(xb, win_ref[:, d_ff + lo:d_ff + lo + chunk], preferred_element_type=F32)
        acts[c] = (gate * jax.nn.sigmoid(gate) * up).astype(BF16)
        finish_group()
        if c >= lag:
            down(c - lag)
    for c in range(max(n_chunks - lag, 0), n_chunks):
        finish_group()
        down(c)
    finish_group()
    y_ref[...] = DEEPNORM_ALPHA * x + FFN_RES * acc_ref[...]


def _ffn(x, w_in, w_out, g1, b1, pool=None, *, seq=None, tm=512, chunk=256, lag=2, ln_rows=64):
    n, d = x.shape
    d_ff = w_out.shape[0]
    n_tiles = n // tm
    assert n % tm == 0 and d_ff % chunk == 0 and w_in.shape == (d, 2 * d_ff)
    assert tm % ln_rows == 0 and tm // ln_rows <= d_ff // chunk and ln_rows >= POOL_HALO
    tile = _nbytes((tm, d), F32)
    row = pl.BlockSpec((1, d), _const2)
    in_specs = [
        pl.BlockSpec((tm, d), lambda i: (jnp.minimum(i, n_tiles - 1), 0)),
        pl.BlockSpec((d, 2 * d_ff), _const2, pipeline_mode=pl.Buffered(1)),
        pl.BlockSpec((d_ff, d), _const2, pipeline_mode=pl.Buffered(1)),
        row, row,
    ]
    args = [x, w_in, w_out, g1, b1]
    scratch = [pltpu.VMEM((tm, d), F32), pltpu.VMEM((tm, d), F32)]
    seq_tiles = None
    pool_bytes = 0
    if pool is not None:
        w_pool, scale, g2, b2 = pool
        assert seq % tm == 0 and max(POOL_WINDOWS) <= POOL_HALO
        assert all(w & (w - 1) == 0 for w in POOL_WINDOWS)
        seq_tiles = seq // tm
        in_specs += [pl.BlockSpec(w_pool.shape, lambda i: (0, 0, 0)), row, row, row]
        args += [w_pool, scale, g2, b2]
        scratch.append(pltpu.VMEM((POOL_HALO, d), F32))
        pool_bytes = 2 * _nbytes(w_pool.shape, BF16)
    vmem = _vmem_limit(_nbytes(w_in.shape, BF16), _nbytes(w_out.shape, BF16), pool_bytes,
                       4 * tile,
                       2 * tile,
                       3 * tile)
    return pl.pallas_call(
        functools.partial(_ffn_body, d_ff=d_ff, chunk=chunk, lag=lag, ln_rows=ln_rows,
                          seq_tiles=seq_tiles),
        out_shape=jax.ShapeDtypeStruct((n, d), F32),
        grid=(n_tiles + 1,),
        in_specs=in_specs,
        out_specs=pl.BlockSpec((tm, d), lambda i: (jnp.maximum(i - 1, 0), 0)),
        scratch_shapes=scratch,
        compiler_params=pltpu.CompilerParams(dimension_semantics=("arbitrary",),
                                             vmem_limit_bytes=vmem),
        name="ffn_ln" if pool is None else "ffn_ln_pool_ln",
    )(*args)


_NT_DIMS = (((1,), (1,)), ((), ()))


def _bf16_pieces(value, n):
    pieces = []
    for _ in range(n):
        piece = float(ml_dtypes.bfloat16(value))
        pieces.append(piece)
        value -= piece
    return pieces


LOG2E_PIECES = _bf16_pieces(math.log2(math.e), 3)
N_BIAS = 2 * len(LOG2E_PIECES)
V_ROWS_PAD = 16


def _qkv_body(x_ref, wqT_ref, wk_ref, wvT_ref, q1T_ref, q2T_ref, k1_ref, k2_ref, vT_ref,
              *, head_dim, scale):
    t = pl.program_id(1)
    ts = x_ref.shape[0]
    width = 2 * head_dim
    n_heads = vT_ref.shape[2] // (width + V_ROWS_PAD)
    xb = x_ref[...].astype(BF16)
    qT = lax.dot_general(wqT_ref[...], xb, _NT_DIMS, preferred_element_type=F32)
    qT = qT * (scale * math.log2(math.e))
    vT = lax.dot_general(wvT_ref[...], xb, _NT_DIMS, preferred_element_type=F32)
    kk = jnp.dot(xb, wk_ref[...], preferred_element_type=F32)

    rr = lax.broadcasted_iota(jnp.int32, (head_dim, ts), 0)
    pair = rr >> 1
    log2e_rows = jnp.where(rr < N_BIAS,
                           jnp.where(pair == 0, LOG2E_PIECES[0],
                                     jnp.where(pair == 1, LOG2E_PIECES[1], LOG2E_PIECES[2])),
                           0.0).astype(BF16)
    ones_rows = (lax.broadcasted_iota(jnp.int32, (V_ROWS_PAD, ts), 0) == 0).astype(BF16)

    lane = lax.broadcasted_iota(jnp.int32, (1, width), 1)
    pos = t * ts + lax.broadcasted_iota(jnp.int32, (ts, 1), 0)
    hi = ((pos >> POS_SPLIT_BITS) << POS_SPLIT_BITS).astype(F32)
    lo = (pos & ((1 << POS_SPLIT_BITS) - 1)).astype(F32)
    hi_lo = 0.5 * jnp.where((lane & 1) == 0, hi, lo)
    bias1 = jnp.where((lane >= head_dim) & (lane < head_dim + N_BIAS), hi_lo, 0.0)
    bias2 = jnp.where(lane < N_BIAS, hi_lo, 0.0)
    first_half = lane < head_dim

    for h in range(n_heads):
        lo_r, mid_r, hi_r = h * width, h * width + head_dim, (h + 1) * width
        q1T_ref[0, 0, lo_r:mid_r, :] = qT[lo_r:mid_r, :].astype(BF16)
        q1T_ref[0, 0, mid_r:hi_r, :] = log2e_rows
        q2T_ref[0, 0, lo_r:mid_r, :] = log2e_rows
        q2T_ref[0, 0, mid_r:hi_r, :] = qT[mid_r:hi_r, :].astype(BF16)

        base = h * (width + V_ROWS_PAD)
        vT_ref[0, 0, base:base + width, :] = vT[lo_r:hi_r, :].astype(BF16)
        vT_ref[0, 0, base + width:base + width + V_ROWS_PAD, :] = ones_rows

        k_h = kk[:, lo_r:hi_r]
        head_scale = 2.0 ** -h
        k1_ref[0, :, lo_r:hi_r] = jnp.where(first_half, k_h, bias1 * head_scale).astype(BF16)
        k2_ref[0, :, lo_r:hi_r] = jnp.where(first_half, bias2 * head_scale, k_h).astype(BF16)


def _qkv_proj(x, wqT, wk, wvT, *, batch, seq, ts, head_dim, scale):
    n, d = x.shape
    hw = wk.shape[1]
    width = 2 * head_dim
    n_heads = hw // width
    st = seq // ts
    assert seq % ts == 0 and seq <= MAX_SEQ and N_BIAS <= head_dim
    tile = _nbytes((ts, hw), F32)
    vmem = _vmem_limit(2 * 3 * _nbytes(wk.shape, BF16), 2 * _nbytes((ts, d), F32),
                       2 * 6 * _nbytes((ts, hw), BF16), 8 * tile)
    row_out = jax.ShapeDtypeStruct((batch, seq, hw), BF16)
    col_out = jax.ShapeDtypeStruct((batch, st, hw, ts), BF16)
    v_rows = n_heads * (width + V_ROWS_PAD)
    v_out = jax.ShapeDtypeStruct((batch, st, v_rows, ts), BF16)
    row_spec = pl.BlockSpec((1, ts, hw), lambda b, t: (b, t, 0))
    col_spec = pl.BlockSpec((1, 1, hw, ts), lambda b, t: (b, t, 0, 0))
    v_spec = pl.BlockSpec((1, 1, v_rows, ts), lambda b, t: (b, t, 0, 0))
    return pl.pallas_call(
        functools.partial(_qkv_body, head_dim=head_dim, scale=scale),
        out_shape=(col_out, col_out, row_out, row_out, v_out),
        grid=(batch, st),
        in_specs=[
            pl.BlockSpec((ts, d), lambda b, t: (b * st + t, 0)),
            pl.BlockSpec(wqT.shape, lambda b, t: (0, 0)),
            pl.BlockSpec(wk.shape, lambda b, t: (0, 0)),
            pl.BlockSpec(wvT.shape, lambda b, t: (0, 0)),
        ],
        out_specs=(col_spec, col_spec, row_spec, row_spec, v_spec),
        compiler_params=pltpu.CompilerParams(dimension_semantics=("arbitrary", "arbitrary"),
                                             vmem_limit_bytes=vmem),
        name="qkv_proj",
    )(x, wqT, wk, wvT)


def _attn_body(q1T_ref, q2T_ref, k1_ref, k2_ref, vT_ref, lam_ref, g_ref, o_ref,
               acc1, acc2, m1, m2, s_a, s_b, mx_a, mx_b, *, lambda_init):
    ts = q1T_ref.shape[-1]
    tq = 2 * ts
    n_tiles = q1T_ref.shape[1] // 2
    dv = o_ref.shape[-1]
    units = tuple((mp, hf) for mp in (0, 1) for hf in (0, 1))

    lam = (jnp.exp(jnp.sum(lam_ref[0:1, :] * lam_ref[1:2, :], axis=-1, keepdims=True))
           - jnp.exp(jnp.sum(lam_ref[2:3, :] * lam_ref[3:4, :], axis=-1, keepdims=True))
           + lambda_init)

    def scores(i, j, s_buf, mx_buf, mp, hf):
        rows = pl.ds(pl.multiple_of(j * ts, ts), ts)
        k = (k1_ref, k2_ref)[mp][0, rows, :]
        qT = (q1T_ref, q2T_ref)[mp][0, 2 * i + hf]
        s = jnp.dot(k, qT, preferred_element_type=F32)
        s_buf[mp, :, hf * ts:(hf + 1) * ts] = s
        mx_buf[mp, :, hf * ts:(hf + 1) * ts] = jnp.max(s, axis=0, keepdims=True)

    def softmax_pv(j, s_buf, mx_buf, mp, hf, causal=False):
        acc, m = ((acc1, m1), (acc2, m2))[mp]
        cols = slice(hf * ts, (hf + 1) * ts)
        v = vT_ref[0, j]
        s = s_buf[mp, :, cols]
        if causal:
            key_idx = lax.broadcasted_iota(jnp.int32, s.shape, 0)
            qry_idx = lax.broadcasted_iota(jnp.int32, s.shape, 1)
            s = jnp.where(key_idx <= qry_idx, s, -jnp.inf)
            blk_max = jnp.max(s, axis=0, keepdims=True)
        else:
            blk_max = mx_buf[mp, :, cols]
        m_old = m[:, cols]
        m_new = jnp.maximum(m_old, blk_max)
        alpha = jnp.exp2(m_old - m_new)
        p = jnp.exp2(s - m_new).astype(BF16)
        acc[:, cols] = alpha * acc[:, cols] + jnp.dot(v, p, preferred_element_type=F32)
        m[:, cols] = m_new

    def reset_stats():
        for m, acc in ((m1, acc1), (m2, acc2)):
            m[...] = jnp.full(m.shape, -jnp.inf, F32)
            acc[...] = jnp.zeros(acc.shape, F32)

    reset_stats()
    for mp, hf in units:
        scores(0, 0, s_a, mx_a, mp, hf)

    def query_tile(i, carry):
        def block_pair(j):
            for mp, hf in units:
                scores(i, j + 1, s_b, mx_b, mp, hf)
                softmax_pv(j, s_a, mx_a, mp, hf)
            for mp, hf in units:
                scores(i, j + 2, s_a, mx_a, mp, hf)
                softmax_pv(j + 1, s_b, mx_b, mp, hf)

        def four_pairs(t, inner):
            for u in range(4):
                block_pair(8 * t + 2 * u)
            return inner

        lax.fori_loop(0, lax.shift_right_logical(i, 2), four_pairs, 0)
        looped = (i >> 2) << 2

        @pl.when((i & 2) == 2)
        def _():
            block_pair(2 * looped)
            block_pair(2 * looped + 2)

        @pl.when((i & 1) == 1)
        def _():
            block_pair(2 * i - 2)

        for mp in (0, 1):
            scores(i, 2 * i + 1, s_b, mx_b, mp, 1)
            softmax_pv(2 * i, s_a, mx_a, mp, 0, causal=True)
            softmax_pv(2 * i, s_a, mx_a, mp, 1)
        for mp in (0, 1):
            softmax_pv(2 * i + 1, s_b, mx_b, mp, 1, causal=True)

        o1T = acc1[0:dv, :] / acc1[dv:dv + 1, :]
        o2T = acc2[0:dv, :] / acc2[dv:dv + 1, :]
        o = (o1T - lam * o2T).T
        o = o * lax.rsqrt(jnp.mean(o * o, axis=-1, keepdims=True) + RMS_EPS)
        o = o * g_ref[...] * (1.0 - lambda_init)
        o_ref[0, pl.ds(pl.multiple_of(i * tq, tq), tq), :] = o.astype(o_ref.dtype)
        reset_stats()
        nxt = jnp.minimum(i + 1, n_tiles - 1)
        for mp, hf in units:
            scores(nxt, 0, s_a, mx_a, mp, hf)
        return carry

    lax.fori_loop(0, n_tiles, query_tile, 0)


def _diff_attn(q1T, q2T, k1, k2, vT, lam_vecs, subln_g, *, lambda_init):
    batch, st, hw, ts = q1T.shape
    seq = st * ts
    width = hw // N_HEADS
    v_rows = width + V_ROWS_PAD
    tq = 2 * ts
    assert vT.shape == (batch, st, N_HEADS * v_rows, ts) and st % 2 == 0
    resident = 2 * (5 * _nbytes((seq, width), BF16) + _nbytes((st, v_rows, ts), BF16))
    score_buf = pltpu.VMEM((2, ts, tq), F32)
    max_buf = pltpu.VMEM((2, 1, tq), F32)
    acc_buf = pltpu.VMEM((v_rows, tq), F32)
    stat = pltpu.VMEM((1, tq), F32)
    vmem = _vmem_limit(resident,
                       2 * _nbytes((v_rows, tq), F32),
                       2 * _nbytes((2, ts, tq), F32),
                       4 * _nbytes((ts, tq), F32))
    q_spec = pl.BlockSpec((1, st, width, ts), lambda b, h: (b, 0, h, 0))
    k_spec = pl.BlockSpec((1, seq, width), lambda b, h: (b, 0, h))
    return pl.pallas_call(
        functools.partial(_attn_body, lambda_init=lambda_init),
        out_shape=jax.ShapeDtypeStruct((batch, seq, hw), BF16),
        grid=(batch, N_HEADS),
        in_specs=[
            q_spec, q_spec, k_spec, k_spec,
            pl.BlockSpec((1, st, v_rows, ts), lambda b, h: (b, 0, h, 0)),
            pl.BlockSpec(lam_vecs.shape, lambda b, h: (0, 0)),
            pl.BlockSpec((1, width), lambda b, h: (0, 0)),
        ],
        out_specs=pl.BlockSpec((1, seq, width), lambda b, h: (b, 0, h)),
        scratch_shapes=[acc_buf, acc_buf, stat, stat, score_buf, score_buf, max_buf, max_buf],
        compiler_params=pltpu.CompilerParams(
            dimension_semantics=("arbitrary", "arbitrary"), vmem_limit_bytes=vmem),
        name="diff_attn",
    )(q1T, q2T, k1, k2, vT, lam_vecs, subln_g)


def _proj_ln_body(x_ref, a_ref, w_ref, g_ref, b_ref, o_ref, *, sub):
    for r in range(x_ref.shape[0] // sub):
        rows = slice(r * sub, (r + 1) * sub)
        y = jnp.dot(a_ref[rows, :], w_ref[...], preferred_element_type=F32)
        o_ref[rows, :] = _layer_norm(DEEPNORM_ALPHA * x_ref[rows, :] + y, g_ref[...], b_ref[...])


def _proj_ln(x, a, w, g, b, *, tm=1024, sub=256):
    n, d = x.shape
    assert n % tm == 0 and tm % sub == 0
    tile = _nbytes((tm, d), F32)
    vmem = _vmem_limit(_nbytes(w.shape, BF16), 4 * tile, tile, 2 * tile)
    return pl.pallas_call(
        functools.partial(_proj_ln_body, sub=sub),
        out_shape=jax.ShapeDtypeStruct((n, d), F32),
        grid=(n // tm,),
        in_specs=[
            pl.BlockSpec((tm, d), lambda i: (i, 0)),
            pl.BlockSpec((tm, a.shape[1]), lambda i: (i, 0)),
            pl.BlockSpec(w.shape, _const2, pipeline_mode=pl.Buffered(1)),
            pl.BlockSpec((1, d), _const2),
            pl.BlockSpec((1, d), _const2),
        ],
        out_specs=pl.BlockSpec((tm, d), lambda i: (i, 0)),
        compiler_params=pltpu.CompilerParams(dimension_semantics=("arbitrary",),
                                             vmem_limit_bytes=vmem),
        name="proj_ln",
    )(x, a, w, g, b)


def _row(v):
    return v.reshape(1, -1).astype(F32)


def _cast_ffn(w_in, w_out, g, b):
    return w_in.astype(BF16), w_out.astype(BF16), _row(g), _row(b)


def kernel(x, l0_ffn1_w_in, l0_ffn1_w_out, l0_ln1_g, l0_ln1_b, l0_pool_w, l0_pool_scale, l0_ln2_g, l0_ln2_b, l0_ffn2_w_in, l0_ffn2_w_out, l0_ln3_g, l0_ln3_b, l1_ffn1_w_in, l1_ffn1_w_out, l1_ln1_g, l1_ln1_b, l1_w_qkv, l1_lam_q1, l1_lam_k1, l1_lam_q2, l1_lam_k2, l1_subln_g, l1_w_o, l1_ln2_g, l1_ln2_b, l1_ffn2_w_in, l1_ffn2_w_out, l1_ln3_g, l1_ln3_b):
    batch, seq, d = x.shape
    h = x.reshape(batch * seq, d)

    pool = (l0_pool_w.astype(BF16), _row(l0_pool_scale), _row(l0_ln2_g), _row(l0_ln2_b))
    h = _ffn(h, *_cast_ffn(l0_ffn1_w_in, l0_ffn1_w_out, l0_ln1_g, l0_ln1_b), pool, seq=seq)
    h = _ffn(h, *_cast_ffn(l0_ffn2_w_in, l0_ffn2_w_out, l0_ln3_g, l0_ln3_b))

    h = _ffn(h, *_cast_ffn(l1_ffn1_w_in, l1_ffn1_w_out, l1_ln1_g, l1_ln1_b))

    head_dim = d // (2 * N_HEADS)
    qk_w = N_HEADS * head_dim
    assert 2 * head_dim == V7X_LANES and l1_w_qkv.shape == (d, 4 * qk_w + N_HEADS * 2 * head_dim)

    def per_head_pairs(w_a, w_b):
        pair = jnp.stack([w_a.reshape(d, N_HEADS, head_dim), w_b.reshape(d, N_HEADS, head_dim)], 2)
        return pair.reshape(d, N_HEADS * 2 * head_dim)

    w = l1_w_qkv.astype(BF16)
    wqT = per_head_pairs(w[:, :qk_w], w[:, qk_w:2 * qk_w]).T
    wk = per_head_pairs(w[:, 2 * qk_w:3 * qk_w], w[:, 3 * qk_w:4 * qk_w])
    wvT = w[:, 4 * qk_w:].T
    q1T, q2T, k1, k2, vT = _qkv_proj(h, wqT, wk, wvT, batch=batch, seq=seq, ts=512,
                                     head_dim=head_dim, scale=head_dim ** -0.5)
    lambda_init = 0.8 - 0.6 * math.exp(-0.3 * 1)
    lam_vecs = jnp.stack([l1_lam_q1, l1_lam_k1, l1_lam_q2, l1_lam_k2]).astype(F32)
    attn = _diff_attn(q1T, q2T, k1, k2, vT, lam_vecs, _row(l1_subln_g), lambda_init=lambda_init)
    h = _proj_ln(h, attn.reshape(batch * seq, d), l1_w_o.astype(BF16), _row(l1_ln2_g),
                 _row(l1_ln2_b))

    h = _ffn(h, *_cast_ffn(l1_ffn2_w_in, l1_ffn2_w_out, l1_ln3_g, l1_ln3_b))
    return h.reshape(batch, seq, d)
```

```python
import functools
import math

import jax
import jax.numpy as jnp
import ml_dtypes
from jax import lax
from jax.experimental import pallas as pl
from jax.experimental.pallas import tpu as pltpu

F32 = jnp.float32
BF16 = jnp.bfloat16

DEPTH = 2
FFN_RES = 0.5
POOL_WINDOWS = (2, 4, 8, 16)
POOL_HALO = 16
N_HEADS = 8
LN_EPS = 1e-5
RMS_EPS = 1e-5
DEEPNORM_ALPHA = (2.0 * DEPTH) ** 0.25

V7X_LANES = 128
V7X_VMEM_BYTES = 64 * 1024 * 1024
V7X_VMEM_RESERVE_BYTES = 8 * 1024 * 1024

POS_SPLIT_BITS = 6
MAX_SEQ = 1 << (POS_SPLIT_BITS + 8)


def _vmem_limit(*byte_counts):
    need = int(sum(byte_counts))
    return min(need + V7X_VMEM_RESERVE_BYTES, V7X_VMEM_BYTES - V7X_VMEM_RESERVE_BYTES)


def _nbytes(shape, dtype):
    return math.prod(shape) * jnp.dtype(dtype).itemsize


def _layer_norm(y, g, b):
    mu = jnp.mean(y, axis=-1, keepdims=True)
    yc = y - mu
    var = jnp.mean(yc * yc, axis=-1, keepdims=True)
    return yc * lax.rsqrt(var + LN_EPS) * g + b


def _const2(i):
    return (0, 0)


def _zero_after(value):
    bits = pltpu.bitcast(value[0:8, 0:V7X_LANES], jnp.uint32)
    zero = lax.shift_right_logical(lax.shift_right_logical(bits, jnp.uint32(16)), jnp.uint32(16))
    return zero[0:1, 0:1].astype(F32)


def _pool_windows(h, prev, pos0):
    rows, d = h.shape
    ext = jnp.concatenate([prev, h], axis=0)
    pos = pos0 + lax.broadcasted_iota(jnp.int32, (rows, 1), 0)
    gw = d // len(POOL_WINDOWS)
    dgs = []
    for gi, w in enumerate(POOL_WINDOWS):
        cols = slice(gi * gw, (gi + 1) * gw)
        run, span = ext[:, cols], 1
        while span < w:
            run = run + pltpu.roll(run, span, axis=0)
            span *= 2
        wsum = run[POOL_HALO:, :]
        count = jnp.minimum(pos + 1, w).astype(F32)
        dgs.append((wsum / count - h[:, cols]).astype(BF16))
    return dgs


def _pool_project(dgs, w_ref, scale):
    outs = [jnp.dot(dg, w_ref[gi], preferred_element_type=F32) for gi, dg in enumerate(dgs)]
    return jnp.concatenate(outs, axis=-1) * scale


def _ffn_body(*refs, d_ff, chunk, lag, ln_rows, seq_tiles):
    if seq_tiles is None:
        x_ref, win_ref, wout_ref, g1_ref, b1_ref, o_ref, acc_ref, y_ref = refs
    else:
        (x_ref, win_ref, wout_ref, g1_ref, b1_ref, wp_ref, sp_ref, g2_ref, b2_ref,
         o_ref, acc_ref, y_ref, halo_ref) = refs
    tm = x_ref.shape[0]
    n_chunks = d_ff // chunk
    n_groups = tm // ln_rows
    step = pl.program_id(0)

    @pl.when(step == 0)
    def _():
        y_ref[...] = jnp.zeros(y_ref.shape, F32)
        if seq_tiles is not None:
            halo_ref[...] = jnp.zeros(halo_ref.shape, F32)

    x = x_ref[...]
    xb = x.astype(BF16)
    acts = {}
    pending = []
    prev_rows = [None]

    def start_group(k, anchor):
        rows = slice(k * ln_rows, (k + 1) * ln_rows)
        h = _layer_norm(y_ref[rows, :] + _zero_after(anchor), g1_ref[...], b1_ref[...])
        if seq_tiles is None:
            o_ref[rows, :] = h
            return
        t_prev = (step + (seq_tiles - 1)) % seq_tiles
        if k == 0:
            prev = jnp.where(t_prev == 0, 0.0, halo_ref[...])
        else:
            prev = prev_rows[0]
        prev_rows[0] = h[ln_rows - POOL_HALO:, :]
        if k == n_groups - 1:
            halo_ref[...] = prev_rows[0]
        pending.append((rows, h, _pool_windows(h, prev, t_prev * tm + k * ln_rows)))

    def finish_group():
        if pending:
            rows, h, dgs = pending.pop()
            mix = _pool_project(dgs, wp_ref, sp_ref[...])
            o_ref[rows, :] = _layer_norm(DEEPNORM_ALPHA * h + mix, g2_ref[...], b2_ref[...])

    def down(c):
        lo = c * chunk
        part = jnp.dot(acts.pop(c), wout_ref[lo:lo + chunk, :], preferred_element_type=F32)
        if c == 0:
            acc_ref[...] = part
        else:
            acc_ref[...] += part
        if c < n_groups:
            start_group(c, part)

    for c in range(n_chunks):
        lo = c * chunk
        gate = jnp.dot(xb, win_ref[:, lo:lo + chunk], preferred_element_type=F32)
        up = jnp.dot(xb, win_ref[:, d_ff + lo:d_ff + lo + chunk], preferred_element_type=F32)
        acts[c] = (gate * jax.nn.sigmoid(gate) * up).astype(BF16)
        finish_group()
        if c >= lag:
            down(c - lag)
    for c in range(max(n_chunks - lag, 0), n_chunks):
        finish_group()
        down(c)
    finish_group()
    y_ref[...] = DEEPNORM_ALPHA * x + FFN_RES * acc_ref[...]


def _ffn(x, w_in, w_out, g1, b1, pool=None, *, seq=None, tm=512, chunk=256, lag=2, ln_rows=128):
    n, d = x.shape
    d_ff = w_out.shape[0]
    n_tiles = n // tm
    assert n % tm == 0 and d_ff % chunk == 0 and w_in.shape == (d, 2 * d_ff)
    assert tm % ln_rows == 0 and tm // ln_rows <= d_ff // chunk and ln_rows >= POOL_HALO
    tile = _nbytes((tm, d), F32)
    row = pl.BlockSpec((1, d), _const2)
    in_specs = [
        pl.BlockSpec((tm, d), lambda i: (jnp.minimum(i, n_tiles - 1), 0)),
        pl.BlockSpec((d, 2 * d_ff), _const2, pipeline_mode=pl.Buffered(1)),
        pl.BlockSpec((d_ff, d), _const2, pipeline_mode=pl.Buffered(1)),
        row, row,
    ]
    args = [x, w_in, w_out, g1, b1]
    scratch = [pltpu.VMEM((tm, d), F32), pltpu.VMEM((tm, d), F32)]
    seq_tiles = None
    pool_bytes = 0
    if pool is not None:
        w_pool, scale, g2, b2 = pool
        assert seq % tm == 0 and max(POOL_WINDOWS) <= POOL_HALO
        assert all(w & (w - 1) == 0 for w in POOL_WINDOWS)
        seq_tiles = seq // tm
        in_specs += [pl.BlockSpec(w_pool.shape, lambda i: (0, 0, 0)), row, row, row]
        args += [w_pool, scale, g2, b2]
        scratch.append(pltpu.VMEM((POOL_HALO, d), F32))
        pool_bytes = 2 * _nbytes(w_pool.shape, BF16)
    vmem = _vmem_limit(_nbytes(w_in.shape, BF16), _nbytes(w_out.shape, BF16), pool_bytes,
                       4 * tile,
                       2 * tile,
                       3 * tile)
    return pl.pallas_call(
        functools.partial(_ffn_body, d_ff=d_ff, chunk=chunk, lag=lag, ln_rows=ln_rows,
                          seq_tiles=seq_tiles),
        out_shape=jax.ShapeDtypeStruct((n, d), F32),
        grid=(n_tiles + 1,),
        in_specs=in_specs,
        out_specs=pl.BlockSpec((tm, d), lambda i: (jnp.maximum(i - 1, 0), 0)),
        scratch_shapes=scratch,
        compiler_params=pltpu.CompilerParams(dimension_semantics=("arbitrary",),
                                             vmem_limit_bytes=vmem),
        name="ffn_ln" if pool is None else "ffn_ln_pool_ln",
    )(*args)


_NT_DIMS = (((1,), (1,)), ((), ()))


def _bf16_pieces(value, n):
    pieces = []
    for _ in range(n):
        piece = float(ml_dtypes.bfloat16(value))
        pieces.append(piece)
        value -= piece
    return pieces


LOG2E_PIECES = _bf16_pieces(math.log2(math.e), 3)
N_BIAS = 2 * len(LOG2E_PIECES)
V_ROWS_PAD = 16


def _qkv_body(x_ref, wqT_ref, wk_ref, wvT_ref, q1T_ref, q2T_ref, k1_ref, k2_ref, vT_ref,
              *, head_dim, scale):
    t = pl.program_id(1)
    ts = x_ref.shape[0]
    width = 2 * head_dim
    n_heads = vT_ref.shape[2] // (width + V_ROWS_PAD)
    xb = x_ref[...].astype(BF16)
    qT = lax.dot_general(wqT_ref[...], xb, _NT_DIMS, preferred_element_type=F32)
    qT = qT * (scale * math.log2(math.e))
    vT = lax.dot_general(wvT_ref[...], xb, _NT_DIMS, preferred_element_type=F32)
    kk = jnp.dot(xb, wk_ref[...], preferred_element_type=F32)

    rr = lax.broadcasted_iota(jnp.int32, (head_dim, ts), 0)
    pair = rr >> 1
    log2e_rows = jnp.where(rr < N_BIAS,
                           jnp.where(pair == 0, LOG2E_PIECES[0],
                                     jnp.where(pair == 1, LOG2E_PIECES[1], LOG2E_PIECES[2])),
                           0.0).astype(BF16)
    ones_rows = (lax.broadcasted_iota(jnp.int32, (V_ROWS_PAD, ts), 0) == 0).astype(BF16)

    lane = lax.broadcasted_iota(jnp.int32, (1, width), 1)
    pos = t * ts + lax.broadcasted_iota(jnp.int32, (ts, 1), 0)
    hi = ((pos >> POS_SPLIT_BITS) << POS_SPLIT_BITS).astype(F32)
    lo = (pos & ((1 << POS_SPLIT_BITS) - 1)).astype(F32)
    hi_lo = 0.5 * jnp.where((lane & 1) == 0, hi, lo)
    bias1 = jnp.where((lane >= head_dim) & (lane < head_dim + N_BIAS), hi_lo, 0.0)
    bias2 = jnp.where(lane < N_BIAS, hi_lo, 0.0)
    first_half = lane < head_dim

    for h in range(n_heads):
        lo_r, mid_r, hi_r = h * width, h * width + head_dim, (h + 1) * width
        q1T_ref[0, 0, lo_r:mid_r, :] = qT[lo_r:mid_r, :].astype(BF16)
        q1T_ref[0, 0, mid_r:hi_r, :] = log2e_rows
        q2T_ref[0, 0, lo_r:mid_r, :] = log2e_rows
        q2T_ref[0, 0, mid_r:hi_r, :] = qT[mid_r:hi_r, :].astype(BF16)

        base = h * (width + V_ROWS_PAD)
        vT_ref[0, 0, base:base + width, :] = vT[lo_r:hi_r, :].astype(BF16)
        vT_ref[0, 0, base + width:base + width + V_ROWS_PAD, :] = ones_rows

        k_h = kk[:, lo_r:hi_r]
        head_scale = 2.0 ** -h
        k1_ref[0, :, lo_r:hi_r] = jnp.where(first_half, k_h, bias1 * head_scale).astype(BF16)
        k2_ref[0, :, lo_r:hi_r] = jnp.where(first_half, bias2 * head_scale, k_h).astype(BF16)


def _qkv_proj(x, wqT, wk, wvT, *, batch, seq, ts, head_dim, scale):
    n, d = x.shape
    hw = wk.shape[1]
    width = 2 * head_dim
    n_heads = hw // width
    st = seq // ts
    assert seq % ts == 0 and seq <= MAX_SEQ and N_BIAS <= head_dim
    tile = _nbytes((ts, hw), F32)
    vmem = _vmem_limit(2 * 3 * _nbytes(wk.shape, BF16), 2 * _nbytes((ts, d), F32),
                       2 * 6 * _nbytes((ts, hw), BF16), 8 * tile)
    row_out = jax.ShapeDtypeStruct((batch, seq, hw), BF16)
    col_out = jax.ShapeDtypeStruct((batch, st, hw, ts), BF16)
    v_rows = n_heads * (width + V_ROWS_PAD)
    v_out = jax.ShapeDtypeStruct((batch, st, v_rows, ts), BF16)
    row_spec = pl.BlockSpec((1, ts, hw), lambda b, t: (b, t, 0))
    col_spec = pl.BlockSpec((1, 1, hw, ts), lambda b, t: (b, t, 0, 0))
    v_spec = pl.BlockSpec((1, 1, v_rows, ts), lambda b, t: (b, t, 0, 0))
    return pl.pallas_call(
        functools.partial(_qkv_body, head_dim=head_dim, scale=scale),
        out_shape=(col_out, col_out, row_out, row_out, v_out),
        grid=(batch, st),
        in_specs=[
            pl.BlockSpec((ts, d), lambda b, t: (b * st + t, 0)),
            pl.BlockSpec(wqT.shape, lambda b, t: (0, 0)),
            pl.BlockSpec(wk.shape, lambda b, t: (0, 0)),
            pl.BlockSpec(wvT.shape, lambda b, t: (0, 0)),
        ],
        out_specs=(col_spec, col_spec, row_spec, row_spec, v_spec),
        compiler_params=pltpu.CompilerParams(dimension_semantics=("arbitrary", "arbitrary"),
                                             vmem_limit_bytes=vmem),
        name="qkv_proj",
    )(x, wqT, wk, wvT)


def _attn_body(q1T_ref, q2T_ref, k1_ref, k2_ref, vT_ref, lam_ref, g_ref, o_ref,
               acc1, acc2, m1, m2, s_a, s_b, mx_a, mx_b, *, lambda_init):
    ts = q1T_ref.shape[-1]
    tq = 2 * ts
    n_tiles = q1T_ref.shape[1] // 2
    dv = o_ref.shape[-1]
    units = tuple((mp, hf) for mp in (0, 1) for hf in (0, 1))

    lam = (jnp.exp(jnp.sum(lam_ref[0:1, :] * lam_ref[1:2, :], axis=-1, keepdims=True))
           - jnp.exp(jnp.sum(lam_ref[2:3, :] * lam_ref[3:4, :], axis=-1, keepdims=True))
           + lambda_init)

    def scores(i, j, s_buf, mx_buf, mp, hf):
        rows = pl.ds(pl.multiple_of(j * ts, ts), ts)
        k = (k1_ref, k2_ref)[mp][0, rows, :]
        qT = (q1T_ref, q2T_ref)[mp][0, 2 * i + hf]
        s = jnp.dot(k, qT, preferred_element_type=F32)
        s_buf[mp, :, hf * ts:(hf + 1) * ts] = s
        mx_buf[mp, :, hf * ts:(hf + 1) * ts] = jnp.max(s, axis=0, keepdims=True)

    def softmax_pv(j, s_buf, mx_buf, mp, hf, causal=False):
        acc, m = ((acc1, m1), (acc2, m2))[mp]
        cols = slice(hf * ts, (hf + 1) * ts)
        v = vT_ref[0, j]
        s = s_buf[mp, :, cols]
        if causal:
            key_idx = lax.broadcasted_iota(jnp.int32, s.shape, 0)
            qry_idx = lax.broadcasted_iota(jnp.int32, s.shape, 1)
            s = jnp.where(key_idx <= qry_idx, s, -jnp.inf)
            blk_max = jnp.max(s, axis=0, keepdims=True)
        else:
            blk_max = mx_buf[mp, :, cols]
        m_old = m[:, cols]
        m_new = jnp.maximum(m_old, blk_max)
        alpha = jnp.exp2(m_old - m_new)
        p = jnp.exp2(s - m_new).astype(BF16)
        acc[:, cols] = alpha * acc[:, cols] + jnp.dot(v, p, preferred_element_type=F32)
        m[:, cols] = m_new

    def reset_stats():
        for m, acc in ((m1, acc1), (m2, acc2)):
            m[...] = jnp.full(m.shape, -jnp.inf, F32)
            acc[...] = jnp.zeros(acc.shape, F32)

    reset_stats()
    for mp, hf in units:
        scores(0, 0, s_a, mx_a, mp, hf)

    def query_tile(i, carry):
        def block_pair(j):
            for mp, hf in units:
                scores(i, j + 1, s_b, mx_b, mp, hf)
                softmax_pv(j, s_a, mx_a, mp, hf)
            for mp, hf in units:
                scores(i, j + 2, s_a, mx_a, mp, hf)
                softmax_pv(j + 1, s_b, mx_b, mp, hf)

        def four_pairs(t, inner):
            for u in range(4):
                block_pair(8 * t + 2 * u)
            return inner

        lax.fori_loop(0, lax.shift_right_logical(i, 2), four_pairs, 0)
        looped = (i >> 2) << 2

        @pl.when((i & 2) == 2)
        def _():
            block_pair(2 * looped)
            block_pair(2 * looped + 2)

        @pl.when((i & 1) == 1)
        def _():
            block_pair(2 * i - 2)

        for mp in (0, 1):
            scores(i, 2 * i + 1, s_b, mx_b, mp, 1)
            softmax_pv(2 * i, s_a, mx_a, mp, 0, causal=True)
            softmax_pv(2 * i, s_a, mx_a, mp, 1)
        for mp in (0, 1):
            softmax_pv(2 * i + 1, s_b, mx_b, mp, 1, causal=True)

        o1T = acc1[0:dv, :] / acc1[dv:dv + 1, :]
        o2T = acc2[0:dv, :] / acc2[dv:dv + 1, :]
        o = (o1T - lam * o2T).T
        o = o * lax.rsqrt(jnp.mean(o * o, axis=-1, keepdims=True) + RMS_EPS)
        o = o * g_ref[...] * (1.0 - lambda_init)
        o_ref[0, pl.ds(pl.multiple_of(i * tq, tq), tq), :] = o.astype(o_ref.dtype)
        reset_stats()
        nxt = jnp.minimum(i + 1, n_tiles - 1)
        for mp, hf in units:
            scores(nxt, 0, s_a, mx_a, mp, hf)
        return carry

    lax.fori_loop(0, n_tiles, query_tile, 0)


def _diff_attn(q1T, q2T, k1, k2, vT, lam_vecs, subln_g, *, lambda_init):
    batch, st, hw, ts = q1T.shape
    seq = st * ts
    width = hw // N_HEADS
    v_rows = width + V_ROWS_PAD
    tq = 2 * ts
    assert vT.shape == (batch, st, N_HEADS * v_rows, ts) and st % 2 == 0
    resident = 2 * (5 * _nbytes((seq, width), BF16) + _nbytes((st, v_rows, ts), BF16))
    score_buf = pltpu.VMEM((2, ts, tq), F32)
    max_buf = pltpu.VMEM((2, 1, tq), F32)
    acc_buf = pltpu.VMEM((v_rows, tq), F32)
    stat = pltpu.VMEM((1, tq), F32)
    vmem = _vmem_limit(resident,
                       2 * _nbytes((v_rows, tq), F32),
                       2 * _nbytes((2, ts, tq), F32),
                       4 * _nbytes((ts, tq), F32))
    q_spec = pl.BlockSpec((1, st, width, ts), lambda b, h: (b, 0, h, 0))
    k_spec = pl.BlockSpec((1, seq, width), lambda b, h: (b, 0, h))
    return pl.pallas_call(
        functools.partial(_attn_body, lambda_init=lambda_init),
        out_shape=jax.ShapeDtypeStruct((batch, seq, hw), BF16),
        grid=(batch, N_HEADS),
        in_specs=[
            q_spec, q_spec, k_spec, k_spec,
            pl.BlockSpec((1, st, v_rows, ts), lambda b, h: (b, 0, h, 0)),
            pl.BlockSpec(lam_vecs.shape, lambda b, h: (0, 0)),
            pl.BlockSpec((1, width), lambda b, h: (0, 0)),
        ],
        out_specs=pl.BlockSpec((1, seq, width), lambda b, h: (b, 0, h)),
        scratch_shapes=[acc_buf, acc_buf, stat, stat, score_buf, score_buf, max_buf, max_buf],
        compiler_params=pltpu.CompilerParams(
            dimension_semantics=("arbitrary", "arbitrary"), vmem_limit_bytes=vmem),
        name="diff_attn",
    )(q1T, q2T, k1, k2, vT, lam_vecs, subln_g)


def _proj_ln_body(x_ref, a_ref, w_ref, g_ref, b_ref, o_ref, *, sub):
    for r in range(x_ref.shape[0] // sub):
        rows = slice(r * sub, (r + 1) * sub)
        y = jnp.dot(a_ref[rows, :], w_ref[...], preferred_element_type=F32)
        o_ref[rows, :] = _layer_norm(DEEPNORM_ALPHA * x_ref[rows, :] + y, g_ref[...], b_ref[...])


def _proj_ln(x, a, w, g, b, *, tm=1024, sub=256):
    n, d = x.shape
    assert n % tm == 0 and tm % sub == 0
    tile = _nbytes((tm, d), F32)
    vmem = _vmem_limit(_nbytes(w.shape, BF16), 4 * tile, tile, 2 * tile)
    return pl.pallas_call(
        functools.partial(_proj_ln_body, sub=sub),
        out_shape=jax.ShapeDtypeStruct((n, d), F32),
        grid=(n // tm,),
        in_specs=[
            pl.BlockSpec((tm, d), lambda i: (i, 0)),
            pl.BlockSpec((tm, a.shape[1]), lambda i: (i, 0)),
            pl.BlockSpec(w.shape, _const2, pipeline_mode=pl.Buffered(1)),
            pl.BlockSpec((1, d), _const2),
            pl.BlockSpec((1, d), _const2),
        ],
        out_specs=pl.BlockSpec((tm, d), lambda i: (i, 0)),
        compiler_params=pltpu.CompilerParams(dimension_semantics=("arbitrary",),
                                             vmem_limit_bytes=vmem),
        name="proj_ln",
    )(x, a, w, g, b)


def _row(v):
    return v.reshape(1, -1).astype(F32)


def _cast_ffn(w_in, w_out, g, b):
    return w_in.astype(BF16), w_out.astype(BF16), _row(g), _row(b)


def kernel(x, l0_ffn1_w_in, l0_ffn1_w_out, l0_ln1_g, l0_ln1_b, l0_pool_w, l0_pool_scale, l0_ln2_g, l0_ln2_b, l0_ffn2_w_in, l0_ffn2_w_out, l0_ln3_g, l0_ln3_b, l1_ffn1_w_in, l1_ffn1_w_out, l1_ln1_g, l1_ln1_b, l1_w_qkv, l1_lam_q1, l1_lam_k1, l1_lam_q2, l1_lam_k2, l1_subln_g, l1_w_o, l1_ln2_g, l1_ln2_b, l1_ffn2_w_in, l1_ffn2_w_out, l1_ln3_g, l1_ln3_b):
    batch, seq, d = x.shape
    h = x.reshape(batch * seq, d)

    pool = (l0_pool_w.astype(BF16), _row(l0_pool_scale), _row(l0_ln2_g), _row(l0_ln2_b))
    h = _ffn(h, *_cast_ffn(l0_ffn1_w_in, l0_ffn1_w_out, l0_ln1_g, l0_ln1_b), pool, seq=seq)
    h = _ffn(h, *_cast_ffn(l0_ffn2_w_in, l0_ffn2_w_out, l0_ln3_g, l0_ln3_b))

    h = _ffn(h, *_cast_ffn(l1_ffn1_w_in, l1_ffn1_w_out, l1_ln1_g, l1_ln1_b))

    head_dim = d // (2 * N_HEADS)
    qk_w = N_HEADS * head_dim
    assert 2 * head_dim == V7X_LANES and l1_w_qkv.shape == (d, 4 * qk_w + N_HEADS * 2 * head_dim)

    def per_head_pairs(w_a, w_b):
        pair = jnp.stack([w_a.reshape(d, N_HEADS, head_dim), w_b.reshape(d, N_HEADS, head_dim)], 2)
        return pair.reshape(d, N_HEADS * 2 * head_dim)

    w = l1_w_qkv.astype(BF16)
    wqT = per_head_pairs(w[:, :qk_w], w[:, qk_w:2 * qk_w]).T
    wk = per_head_pairs(w[:, 2 * qk_w:3 * qk_w], w[:, 3 * qk_w:4 * qk_w])
    wvT = w[:, 4 * qk_w:].T
    q1T, q2T, k1, k2, vT = _qkv_proj(h, wqT, wk, wvT, batch=batch, seq=seq, ts=512,
                                     head_dim=head_dim, scale=head_dim ** -0.5)
    lambda_init = 0.8 - 0.6 * math.exp(-0.3 * 1)
    lam_vecs = jnp.stack([l1_lam_q1, l1_lam_k1, l1_lam_q2, l1_lam_k2]).astype(F32)
    attn = _diff_attn(q1T, q2T, k1, k2, vT, lam_vecs, _row(l1_subln_g), lambda_init=lambda_init)
    h = _proj_ln(h, attn.reshape(batch * seq, d), l1_w_o.astype(BF16), _row(l1_ln2_g),
                 _row(l1_ln2_b))

    h = _ffn(h, *_cast_ffn(l1_ffn2_w_in, l1_ffn2_w_out, l1_ln3_g, l1_ln3_b))
    return h.reshape(batch, seq, d)
```

```python
import functools
import math

import jax
import jax.numpy as jnp
import ml_dtypes
from jax import lax
from jax.experimental import pallas as pl
from jax.experimental.pallas import tpu as pltpu

F32 = jnp.float32
BF16 = jnp.bfloat16

DEPTH = 2
FFN_RES = 0.5
POOL_WINDOWS = (2, 4, 8, 16)
POOL_HALO = 16
N_HEADS = 8
LN_EPS = 1e-5
RMS_EPS = 1e-5
DEEPNORM_ALPHA = (2.0 * DEPTH) ** 0.25

V7X_LANES = 128
V7X_VMEM_BYTES = 64 * 1024 * 1024
V7X_VMEM_RESERVE_BYTES = 8 * 1024 * 1024

POS_SPLIT_BITS = 6
MAX_SEQ = 1 << (POS_SPLIT_BITS + 8)


def _vmem_limit(*byte_counts):
    need = int(sum(byte_counts))
    return min(need + V7X_VMEM_RESERVE_BYTES, V7X_VMEM_BYTES - V7X_VMEM_RESERVE_BYTES)


def _nbytes(shape, dtype):
    return math.prod(shape) * jnp.dtype(dtype).itemsize


def _layer_norm(y, g, b):
    mu = jnp.mean(y, axis=-1, keepdims=True)
    yc = y - mu
    var = jnp.mean(yc * yc, axis=-1, keepdims=True)
    return yc * lax.rsqrt(var + LN_EPS) * g + b


def _const2(i):
    return (0, 0)


def _zero_after(value):
    bits = pltpu.bitcast(value[0:8, 0:V7X_LANES], jnp.uint32)
    zero = lax.shift_right_logical(lax.shift_right_logical(bits, jnp.uint32(16)), jnp.uint32(16))
    return zero[0:1, 0:1].astype(F32)


def _pool_windows(h, prev, pos0):
    rows, d = h.shape
    ext = jnp.concatenate([prev, h], axis=0)
    pos = pos0 + lax.broadcasted_iota(jnp.int32, (rows, 1), 0)
    gw = d // len(POOL_WINDOWS)
    dgs = []
    for gi, w in enumerate(POOL_WINDOWS):
        cols = slice(gi * gw, (gi + 1) * gw)
        run, span = ext[:, cols], 1
        while span < w:
            run = run + pltpu.roll(run, span, axis=0)
            span *= 2
        wsum = run[POOL_HALO:, :]
        count = jnp.minimum(pos + 1, w).astype(F32)
        dgs.append((wsum / count - h[:, cols]).astype(BF16))
    return dgs


def _pool_project(dgs, w_ref, scale):
    outs = [jnp.dot(dg, w_ref[gi], preferred_element_type=F32) for gi, dg in enumerate(dgs)]
    return jnp.concatenate(outs, axis=-1) * scale


def _ffn_body(*refs, d_ff, chunk, lag, ln_rows, seq_tiles):
    if seq_tiles is None:
        x_ref, win_ref, wout_ref, g1_ref, b1_ref, o_ref, acc_ref, y_ref = refs
    else:
        (x_ref, win_ref, wout_ref, g1_ref, b1_ref, wp_ref, sp_ref, g2_ref, b2_ref,
         o_ref, acc_ref, y_ref, halo_ref) = refs
    tm = x_ref.shape[0]
    n_chunks = d_ff // chunk
    n_groups = tm // ln_rows
    step = pl.program_id(0)

    @pl.when(step == 0)
    def _():
        y_ref[...] = jnp.zeros(y_ref.shape, F32)
        if seq_tiles is not None:
            halo_ref[...] = jnp.zeros(halo_ref.shape, F32)

    x = x_ref[...]
    xb = x.astype(BF16)
    acts = {}
    pending = []
    prev_rows = [None]

    def start_group(k, anchor):
        rows = slice(k * ln_rows, (k + 1) * ln_rows)
        h = _layer_norm(y_ref[rows, :] + _zero_after(anchor), g1_ref[...], b1_ref[...])
        if seq_tiles is None:
            o_ref[rows, :] = h
            return
        t_prev = (step + (seq_tiles - 1)) % seq_tiles
        if k == 0:
            prev = jnp.where(t_prev == 0, 0.0, halo_ref[...])
        else:
            prev = prev_rows[0]
        prev_rows[0] = h[ln_rows - POOL_HALO:, :]
        if k == n_groups - 1:
            halo_ref[...] = prev_rows[0]
        pending.append((rows, h, _pool_windows(h, prev, t_prev * tm + k * ln_rows)))

    def finish_group():
        if pending:
            rows, h, dgs = pending.pop()
            mix = _pool_project(dgs, wp_ref, sp_ref[...])
            o_ref[rows, :] = _layer_norm(DEEPNORM_ALPHA * h + mix, g2_ref[...], b2_ref[...])

    def down(c):
        lo = c * chunk
        part = jnp.dot(acts.pop(c), wout_ref[lo:lo + chunk, :], preferred_element_type=F32)
        if c == 0:
            acc_ref[...] = part
        else:
            acc_ref[...] += part
        if c < n_groups:
            start_group(c, part)

    for c in range(n_chunks):
        lo = c * chunk
        gate = jnp.dot(xb, win_ref[:, lo:lo + chunk], preferred_element_type=F32)
        up = jnp.dot(xb, win_ref[:, d_ff + lo:d_ff + lo + chunk], preferred_element_type=F32)
        acts[c] = (gate * jax.nn.sigmoid(gate) * up).astype(BF16)
        finish_group()
        if c >= lag:
            down(c - lag)
    for c in range(max(n_chunks - lag, 0), n_chunks):
        finish_group()
        down(c)
    finish_group()
    y_ref[...] = DEEPNORM_ALPHA * x + FFN_RES * acc_ref[...]


def _ffn(x, w_in, w_out, g1, b1, pool=None, *, seq=None, tm=512, chunk=256, lag=2):
    ln_rows = 256 if pool is not None else 64
    n, d = x.shape
    d_ff = w_out.shape[0]
    n_tiles = n // tm
    assert n % tm == 0 and d_ff % chunk == 0 and w_in.shape == (d, 2 * d_ff)
    assert tm % ln_rows == 0 and tm // ln_rows <= d_ff // chunk and ln_rows >= POOL_HALO
    tile = _nbytes((tm, d), F32)
    row = pl.BlockSpec((1, d), _const2)
    in_specs = [
        pl.BlockSpec((tm, d), lambda i: (jnp.minimum(i, n_tiles - 1), 0)),
        pl.BlockSpec((d, 2 * d_ff), _const2, pipeline_mode=pl.Buffered(1)),
        pl.BlockSpec((d_ff, d), _const2, pipeline_mode=pl.Buffered(1)),
        row, row,
    ]
    args = [x, w_in, w_out, g1, b1]
    scratch = [pltpu.VMEM((tm, d), F32), pltpu.VMEM((tm, d), F32)]
    seq_tiles = None
    pool_bytes = 0
    if pool is not None:
        w_pool, scale, g2, b2 = pool
        assert seq % tm == 0 and max(POOL_WINDOWS) <= POOL_HALO
        assert all(w & (w - 1) == 0 for w in POOL_WINDOWS)
        seq_tiles = seq // tm
        in_specs += [pl.BlockSpec(w_pool.shape, lambda i: (0, 0, 0)), row, row, row]
        args += [w_pool, scale, g2, b2]
        scratch.append(pltpu.VMEM((POOL_HALO, d), F32))
        pool_bytes = 2 * _nbytes(w_pool.shape, BF16)
    vmem = _vmem_limit(_nbytes(w_in.shape, BF16), _nbytes(w_out.shape, BF16), pool_bytes,
                       4 * tile,
                       2 * tile,
                       3 * tile)
    return pl.pallas_call(
        functools.partial(_ffn_body, d_ff=d_ff, chunk=chunk, lag=lag, ln_rows=ln_rows,
                          seq_tiles=seq_tiles),
        out_shape=jax.ShapeDtypeStruct((n, d), F32),
        grid=(n_tiles + 1,),
        in_specs=in_specs,
        out_specs=pl.BlockSpec((tm, d), lambda i: (jnp.maximum(i - 1, 0), 0)),
        scratch_shapes=scratch,
        compiler_params=pltpu.CompilerParams(dimension_semantics=("arbitrary",),
                                             vmem_limit_bytes=vmem),
        name="ffn_ln" if pool is None else "ffn_ln_pool_ln",
    )(*args)


_NT_DIMS = (((1,), (1,)), ((), ()))


def _bf16_pieces(value, n):
    pieces = []
    for _ in range(n):
        piece = float(ml_dtypes.bfloat16(value))
        pieces.append(piece)
        value -= piece
    return pieces


LOG2E_PIECES = _bf16_pieces(math.log2(math.e), 3)
N_BIAS = 2 * len(LOG2E_PIECES)
V_ROWS_PAD = 16


def _qkv_body(x_ref, wqT_ref, wk_ref, wvT_ref, q1T_ref, q2T_ref, k1_ref, k2_ref, vT_ref,
              *, head_dim, scale):
    t = pl.program_id(1)
    ts = x_ref.shape[0]
    width = 2 * head_dim
    n_heads = vT_ref.shape[2] // (width + V_ROWS_PAD)
    xb = x_ref[...].astype(BF16)
    qT = lax.dot_general(wqT_ref[...], xb, _NT_DIMS, preferred_element_type=F32)
    qT = qT * (scale * math.log2(math.e))
    vT = lax.dot_general(wvT_ref[...], xb, _NT_DIMS, preferred_element_type=F32)
    kk = jnp.dot(xb, wk_ref[...], preferred_element_type=F32)

    rr = lax.broadcasted_iota(jnp.int32, (head_dim, ts), 0)
    pair = rr >> 1
    log2e_rows = jnp.where(rr < N_BIAS,
                           jnp.where(pair == 0, LOG2E_PIECES[0],
                                     jnp.where(pair == 1, LOG2E_PIECES[1], LOG2E_PIECES[2])),
                           0.0).astype(BF16)
    ones_rows = (lax.broadcasted_iota(jnp.int32, (V_ROWS_PAD, ts), 0) == 0).astype(BF16)

    lane = lax.broadcasted_iota(jnp.int32, (1, width), 1)
    pos = t * ts + lax.broadcasted_iota(jnp.int32, (ts, 1), 0)
    hi = ((pos >> POS_SPLIT_BITS) << POS_SPLIT_BITS).astype(F32)
    lo = (pos & ((1 << POS_SPLIT_BITS) - 1)).astype(F32)
    hi_lo = 0.5 * jnp.where((lane & 1) == 0, hi, lo)
    bias1 = jnp.where((lane >= head_dim) & (lane < head_dim + N_BIAS), hi_lo, 0.0)
    bias2 = jnp.where(lane < N_BIAS, hi_lo, 0.0)
    first_half = lane < head_dim

    for h in range(n_heads):
        lo_r, mid_r, hi_r = h * width, h * width + head_dim, (h + 1) * width
        q1T_ref[0, 0, lo_r:mid_r, :] = qT[lo_r:mid_r, :].astype(BF16)
        q1T_ref[0, 0, mid_r:hi_r, :] = log2e_rows
        q2T_ref[0, 0, lo_r:mid_r, :] = log2e_rows
        q2T_ref[0, 0, mid_r:hi_r, :] = qT[mid_r:hi_r, :].astype(BF16)

        base = h * (width + V_ROWS_PAD)
        vT_ref[0, 0, base:base + width, :] = vT[lo_r:hi_r, :].astype(BF16)
        vT_ref[0, 0, base + width:base + width + V_ROWS_PAD, :] = ones_rows

        k_h = kk[:, lo_r:hi_r]
        head_scale = 2.0 ** -h
        k1_ref[0, :, lo_r:hi_r] = jnp.where(first_half, k_h, bias1 * head_scale).astype(BF16)
        k2_ref[0, :, lo_r:hi_r] = jnp.where(first_half, bias2 * head_scale, k_h).astype(BF16)


def _qkv_proj(x, wqT, wk, wvT, *, batch, seq, ts, head_dim, scale):
    n, d = x.shape
    hw = wk.shape[1]
    width = 2 * head_dim
    n_heads = hw // width
    st = seq // ts
    assert seq % ts == 0 and seq <= MAX_SEQ and N_BIAS <= head_dim
    tile = _nbytes((ts, hw), F32)
    vmem = _vmem_limit(2 * 3 * _nbytes(wk.shape, BF16), 2 * _nbytes((ts, d), F32),
                       2 * 6 * _nbytes((ts, hw), BF16), 8 * tile)
    row_out = jax.ShapeDtypeStruct((batch, seq, hw), BF16)
    col_out = jax.ShapeDtypeStruct((batch, st, hw, ts), BF16)
    v_rows = n_heads * (width + V_ROWS_PAD)
    v_out = jax.ShapeDtypeStruct((batch, st, v_rows, ts), BF16)
    row_spec = pl.BlockSpec((1, ts, hw), lambda b, t: (b, t, 0))
    col_spec = pl.BlockSpec((1, 1, hw, ts), lambda b, t: (b, t, 0, 0))
    v_spec = pl.BlockSpec((1, 1, v_rows, ts), lambda b, t: (b, t, 0, 0))
    return pl.pallas_call(
        functools.partial(_qkv_body, head_dim=head_dim, scale=scale),
        out_shape=(col_out, col_out, row_out, row_out, v_out),
        grid=(batch, st),
        in_specs=[
            pl.BlockSpec((ts, d), lambda b, t: (b * st + t, 0)),
            pl.BlockSpec(wqT.shape, lambda b, t: (0, 0)),
            pl.BlockSpec(wk.shape, lambda b, t: (0, 0)),
            pl.BlockSpec(wvT.shape, lambda b, t: (0, 0)),
        ],
        out_specs=(col_spec, col_spec, row_spec, row_spec, v_spec),
        compiler_params=pltpu.CompilerParams(dimension_semantics=("arbitrary", "arbitrary"),
                                             vmem_limit_bytes=vmem),
        name="qkv_proj",
    )(x, wqT, wk, wvT)


def _attn_body(q1T_ref, q2T_ref, k1_ref, k2_ref, vT_ref, lam_ref, g_ref, o_ref,
               acc1, acc2, m1, m2, s_a, s_b, mx_a, mx_b, *, lambda_init):
    ts = q1T_ref.shape[-1]
    tq = 2 * ts
    n_tiles = q1T_ref.shape[1] // 2
    dv = o_ref.shape[-1]
    units = tuple((mp, hf) for mp in (0, 1) for hf in (0, 1))

    lam = (jnp.exp(jnp.sum(lam_ref[0:1, :] * lam_ref[1:2, :], axis=-1, keepdims=True))
           - jnp.exp(jnp.sum(lam_ref[2:3, :] * lam_ref[3:4, :], axis=-1, keepdims=True))
           + lambda_init)

    def scores(i, j, s_buf, mx_buf, mp, hf):
        rows = pl.ds(pl.multiple_of(j * ts, ts), ts)
        k = (k1_ref, k2_ref)[mp][0, rows, :]
        qT = (q1T_ref, q2T_ref)[mp][0, 2 * i + hf]
        s = jnp.dot(k, qT, preferred_element_type=F32)
        s_buf[mp, :, hf * ts:(hf + 1) * ts] = s
        mx_buf[mp, :, hf * ts:(hf + 1) * ts] = jnp.max(s, axis=0, keepdims=True)

    def softmax_pv(j, s_buf, mx_buf, mp, hf, causal=False):
        acc, m = ((acc1, m1), (acc2, m2))[mp]
        cols = slice(hf * ts, (hf + 1) * ts)
        v = vT_ref[0, j]
        s = s_buf[mp, :, cols]
        if causal:
            key_idx = lax.broadcasted_iota(jnp.int32, s.shape, 0)
            qry_idx = lax.broadcasted_iota(jnp.int32, s.shape, 1)
            s = jnp.where(key_idx <= qry_idx, s, -jnp.inf)
            blk_max = jnp.max(s, axis=0, keepdims=True)
        else:
            blk_max = mx_buf[mp, :, cols]
        m_old = m[:, cols]
        m_new = jnp.maximum(m_old, blk_max)
        alpha = jnp.exp2(m_old - m_new)
        p = jnp.exp2(s - m_new).astype(BF16)
        acc[:, cols] = alpha * acc[:, cols] + jnp.dot(v, p, preferred_element_type=F32)
        m[:, cols] = m_new

    def reset_stats():
        for m, acc in ((m1, acc1), (m2, acc2)):
            m[...] = jnp.full(m.shape, -jnp.inf, F32)
            acc[...] = jnp.zeros(acc.shape, F32)

    reset_stats()
    for mp, hf in units:
        scores(0, 0, s_a, mx_a, mp, hf)

    def query_tile(i, carry):
        def block_pair(j):
            for mp, hf in units:
                scores(i, j + 1, s_b, mx_b, mp, hf)
                softmax_pv(j, s_a, mx_a, mp, hf)
            for mp, hf in units:
                scores(i, j + 2, s_a, mx_a, mp, hf)
                softmax_pv(j + 1, s_b, mx_b, mp, hf)

        def four_pairs(t, inner):
            for u in range(4):
                block_pair(8 * t + 2 * u)
            return inner

        lax.fori_loop(0, lax.shift_right_logical(i, 2), four_pairs, 0)
        looped = (i >> 2) << 2

        @pl.when((i & 2) == 2)
        def _():
            block_pair(2 * looped)
            block_pair(2 * looped + 2)

        @pl.when((i & 1) == 1)
        def _():
            block_pair(2 * i - 2)

        for mp in (0, 1):
            scores(i, 2 * i + 1, s_b, mx_b, mp, 1)
            softmax_pv(2 * i, s_a, mx_a, mp, 0, causal=True)
            softmax_pv(2 * i, s_a, mx_a, mp, 1)
        for mp in (0, 1):
            softmax_pv(2 * i + 1, s_b, mx_b, mp, 1, causal=True)

        o1T = acc1[0:dv, :] / acc1[dv:dv + 1, :]
        o2T = acc2[0:dv, :] / acc2[dv:dv + 1, :]
        o = (o1T - lam * o2T).T
        o = o * lax.rsqrt(jnp.mean(o * o, axis=-1, keepdims=True) + RMS_EPS)
        o = o * g_ref[...] * (1.0 - lambda_init)
        o_ref[0, pl.ds(pl.multiple_of(i * tq, tq), tq), :] = o.astype(o_ref.dtype)
        reset_stats()
        nxt = jnp.minimum(i + 1, n_tiles - 1)
        for mp, hf in units:
            scores(nxt, 0, s_a, mx_a, mp, hf)
        return carry

    lax.fori_loop(0, n_tiles, query_tile, 0)


def _diff_attn(q1T, q2T, k1, k2, vT, lam_vecs, subln_g, *, lambda_init):
    batch, st, hw, ts = q1T.shape
    seq = st * ts
    width = hw // N_HEADS
    v_rows = width + V_ROWS_PAD
    tq = 2 * ts
    assert vT.shape == (batch, st, N_HEADS * v_rows, ts) and st % 2 == 0
    resident = 2 * (5 * _nbytes((seq, width), BF16) + _nbytes((st, v_rows, ts), BF16))
    score_buf = pltpu.VMEM((2, ts, tq), F32)
    max_buf = pltpu.VMEM((2, 1, tq), F32)
    acc_buf = pltpu.VMEM((v_rows, tq), F32)
    stat = pltpu.VMEM((1, tq), F32)
    vmem = _vmem_limit(resident,
                       2 * _nbytes((v_rows, tq), F32),
                       2 * _nbytes((2, ts, tq), F32),
                       4 * _nbytes((ts, tq), F32))
    q_spec = pl.BlockSpec((1, st, width, ts), lambda b, h: (b, 0, h, 0))
    k_spec = pl.BlockSpec((1, seq, width), lambda b, h: (b, 0, h))
    return pl.pallas_call(
        functools.partial(_attn_body, lambda_init=lambda_init),
        out_shape=jax.ShapeDtypeStruct((batch, seq, hw), BF16),
        grid=(batch, N_HEADS),
        in_specs=[
            q_spec, q_spec, k_spec, k_spec,
            pl.BlockSpec((1, st, v_rows, ts), lambda b, h: (b, 0, h, 0)),
            pl.BlockSpec(lam_vecs.shape, lambda b, h: (0, 0)),
            pl.BlockSpec((1, width), lambda b, h: (0, 0)),
        ],
        out_specs=pl.BlockSpec((1, seq, width), lambda b, h: (b, 0, h)),
        scratch_shapes=[acc_buf, acc_buf, stat, stat, score_buf, score_buf, max_buf, max_buf],
        compiler_params=pltpu.CompilerParams(
            dimension_semantics=("arbitrary", "arbitrary"), vmem_limit_bytes=vmem),
        name="diff_attn",
    )(q1T, q2T, k1, k2, vT, lam_vecs, subln_g)


def _proj_ln_body(x_ref, a_ref, w_ref, g_ref, b_ref, o_ref, *, sub):
    for r in range(x_ref.shape[0] // sub):
        rows = slice(r * sub, (r + 1) * sub)
        y = jnp.dot(a_ref[rows, :], w_ref[...], preferred_element_type=F32)
        o_ref[rows, :] = _layer_norm(DEEPNORM_ALPHA * x_ref[rows, :] + y, g_ref[...], b_ref[...])


def _proj_ln(x, a, w, g, b, *, tm=1024, sub=256):
    n, d = x.shape
    assert n % tm == 0 and tm % sub == 0
    tile = _nbytes((tm, d), F32)
    vmem = _vmem_limit(_nbytes(w.shape, BF16), 4 * tile, tile, 2 * tile)
    return pl.pallas_call(
        functools.partial(_proj_ln_body, sub=sub),
        out_shape=jax.ShapeDtypeStruct((n, d), F32),
        grid=(n // tm,),
        in_specs=[
            pl.BlockSpec((tm, d), lambda i: (i, 0)),
            pl.BlockSpec((tm, a.shape[1]), lambda i: (i, 0)),
            pl.BlockSpec(w.shape, _const2, pipeline_mode=pl.Buffered(1)),
            pl.BlockSpec((1, d), _const2),
            pl.BlockSpec((1, d), _const2),
        ],
        out_specs=pl.BlockSpec((tm, d), lambda i: (i, 0)),
        compiler_params=pltpu.CompilerParams(dimension_semantics=("arbitrary",),
                                             vmem_limit_bytes=vmem),
        name="proj_ln",
    )(x, a, w, g, b)


def _row(v):
    return v.reshape(1, -1).astype(F32)


def _cast_ffn(w_in, w_out, g, b):
    return w_in.astype(BF16), w_out.astype(BF16), _row(g), _row(b)


def kernel(x, l0_ffn1_w_in, l0_ffn1_w_out, l0_ln1_g, l0_ln1_b, l0_pool_w, l0_pool_scale, l0_ln2_g, l0_ln2_b, l0_ffn2_w_in, l0_ffn2_w_out, l0_ln3_g, l0_ln3_b, l1_ffn1_w_in, l1_ffn1_w_out, l1_ln1_g, l1_ln1_b, l1_w_qkv, l1_lam_q1, l1_lam_k1, l1_lam_q2, l1_lam_k2, l1_subln_g, l1_w_o, l1_ln2_g, l1_ln2_b, l1_ffn2_w_in, l1_ffn2_w_out, l1_ln3_g, l1_ln3_b):
    batch, seq, d = x.shape
    h = x.reshape(batch * seq, d)

    pool = (l0_pool_w.astype(BF16), _row(l0_pool_scale), _row(l0_ln2_g), _row(l0_ln2_b))
    h = _ffn(h, *_cast_ffn(l0_ffn1_w_in, l0_ffn1_w_out, l0_ln1_g, l0_ln1_b), pool, seq=seq)
    h = _ffn(h, *_cast_ffn(l0_ffn2_w_in, l0_ffn2_w_out, l0_ln3_g, l0_ln3_b))

    h = _ffn(h, *_cast_ffn(l1_ffn1_w_in, l1_ffn1_w_out, l1_ln1_g, l1_ln1_b))

    head_dim = d // (2 * N_HEADS)
    qk_w = N_HEADS * head_dim
    assert 2 * head_dim == V7X_LANES and l1_w_qkv.shape == (d, 4 * qk_w + N_HEADS * 2 * head_dim)

    def per_head_pairs(w_a, w_b):
        pair = jnp.stack([w_a.reshape(d, N_HEADS, head_dim), w_b.reshape(d, N_HEADS, head_dim)], 2)
        return pair.reshape(d, N_HEADS * 2 * head_dim)

    w = l1_w_qkv.astype(BF16)
    wqT = per_head_pairs(w[:, :qk_w], w[:, qk_w:2 * qk_w]).T
    wk = per_head_pairs(w[:, 2 * qk_w:3 * qk_w], w[:, 3 * qk_w:4 * qk_w])
    wvT = w[:, 4 * qk_w:].T
    q1T, q2T, k1, k2, vT = _qkv_proj(h, wqT, wk, wvT, batch=batch, seq=seq, ts=512,
                                     head_dim=head_dim, scale=head_dim ** -0.5)
    lambda_init = 0.8 - 0.6 * math.exp(-0.3 * 1)
    lam_vecs = jnp.stack([l1_lam_q1, l1_lam_k1, l1_lam_q2, l1_lam_k2]).astype(F32)
    attn = _diff_attn(q1T, q2T, k1, k2, vT, lam_vecs, _row(l1_subln_g), lambda_init=lambda_init)
    h = _proj_ln(h, attn.reshape(batch * seq, d), l1_w_o.astype(BF16), _row(l1_ln2_g),
                 _row(l1_ln2_b))

    h = _ffn(h, *_cast_ffn(l1_ffn2_w_in, l1_ffn2_w_out, l1_ln3_g, l1_ln3_b))
    return h.reshape(batch, seq, d)
```

```python
import functools
import math

import jax
import jax.numpy as jnp
import ml_dtypes
from jax import lax
from jax.experimental import pallas as pl
from jax.experimental.pallas import tpu as pltpu

F32 = jnp.float32
BF16 = jnp.bfloat16

DEPTH = 2
FFN_RES = 0.5
POOL_WINDOWS = (2, 4, 8, 16)
POOL_HALO = 16
N_HEADS = 8
LN_EPS = 1e-5
RMS_EPS = 1e-5
DEEPNORM_ALPHA = (2.0 * DEPTH) ** 0.25

V7X_LANES = 128
V7X_VMEM_BYTES = 64 * 1024 * 1024
V7X_VMEM_RESERVE_BYTES = 8 * 1024 * 1024

POS_SPLIT_BITS = 6
MAX_SEQ = 1 << (POS_SPLIT_BITS + 8)


def _vmem_limit(*byte_counts):
    need = int(sum(byte_counts))
    return min(need + V7X_VMEM_RESERVE_BYTES, V7X_VMEM_BYTES - V7X_VMEM_RESERVE_BYTES)


def _nbytes(shape, dtype):
    return math.prod(shape) * jnp.dtype(dtype).itemsize


def _layer_norm(y, g, b):
    mu = jnp.mean(y, axis=-1, keepdims=True)
    yc = y - mu
    var = jnp.mean(yc * yc, axis=-1, keepdims=True)
    return yc * lax.rsqrt(var + LN_EPS) * g + b


def _const2(i):
    return (0, 0)


def _zero_after(value):
    bits = pltpu.bitcast(value[0:8, 0:V7X_LANES], jnp.uint32)
    zero = lax.shift_right_logical(lax.shift_right_logical(bits, jnp.uint32(16)), jnp.uint32(16))
    return zero[0:1, 0:1].astype(F32)


def _pool_windows(h, prev, pos0):
    rows, d = h.shape
    ext = jnp.concatenate([prev, h], axis=0)
    pos = pos0 + lax.broadcasted_iota(jnp.int32, (rows, 1), 0)
    gw = d // len(POOL_WINDOWS)
    dgs = []
    for gi, w in enumerate(POOL_WINDOWS):
        cols = slice(gi * gw, (gi + 1) * gw)
        run, span = ext[:, cols], 1
        while span < w:
            run = run + pltpu.roll(run, span, axis=0)
            span *= 2
        wsum = run[POOL_HALO:, :]
        count = jnp.minimum(pos + 1, w).astype(F32)
        dgs.append((wsum / count - h[:, cols]).astype(BF16))
    return dgs


def _pool_project(dgs, w_ref, scale):
    outs = [jnp.dot(dg, w_ref[gi], preferred_element_type=F32) for gi, dg in enumerate(dgs)]
    return jnp.concatenate(outs, axis=-1) * scale


def _ffn_body(*refs, d_ff, chunk, lag, ln_rows, seq_tiles):
    if seq_tiles is None:
        x_ref, win_ref, wout_ref, g1_ref, b1_ref, o_ref, acc_ref, y_ref = refs
    else:
        (x_ref, win_ref, wout_ref, g1_ref, b1_ref, wp_ref, sp_ref, g2_ref, b2_ref,
         o_ref, acc_ref, y_ref, halo_ref) = refs
    tm = x_ref.shape[0]
    n_chunks = d_ff // chunk
    n_groups = tm // ln_rows
    step = pl.program_id(0)

    @pl.when(step == 0)
    def _():
        y_ref[...] = jnp.zeros(y_ref.shape, F32)
        if seq_tiles is not None:
            halo_ref[...] = jnp.zeros(halo_ref.shape, F32)

    x = x_ref[...]
    xb = x.astype(BF16)
    acts = {}
    pending = []
    prev_rows = [None]

    def start_group(k, anchor):
        rows = slice(k * ln_rows, (k + 1) * ln_rows)
        h = _layer_norm(y_ref[rows, :] + _zero_after(anchor), g1_ref[...], b1_ref[...])
        if seq_tiles is None:
            o_ref[rows, :] = h
            return
        t_prev = (step + (seq_tiles - 1)) % seq_tiles
        if k == 0:
            prev = jnp.where(t_prev == 0, 0.0, halo_ref[...])
        else:
            prev = prev_rows[0]
        prev_rows[0] = h[ln_rows - POOL_HALO:, :]
        if k == n_groups - 1:
            halo_ref[...] = prev_rows[0]
        pending.append((rows, h, _pool_windows(h, prev, t_prev * tm + k * ln_rows)))

    def finish_group():
        if pending:
            rows, h, dgs = pending.pop()
            mix = _pool_project(dgs, wp_ref, sp_ref[...])
            o_ref[rows, :] = _layer_norm(DEEPNORM_ALPHA * h + mix, g2_ref[...], b2_ref[...])

    def down(c):
        lo = c * chunk
        part = jnp.dot(acts.pop(c), wout_ref[lo:lo + chunk, :], preferred_element_type=F32)
        if c == 0:
            acc_ref[...] = part
        else:
            acc_ref[...] += part
        if c < n_groups:
            start_group(c, part)

    for c in range(n_chunks):
        lo = c * chunk
        gate = jnp.dot(xb, win_ref[:, lo:lo + chunk], preferred_element_type=F32)
        up = jnp.dot(xb, win_ref[:, d_ff + lo:d_ff + lo + chunk], preferred_element_type=F32)
        acts[c] = (gate * jax.nn.sigmoid(gate) * up).astype(BF16)
        finish_group()
        if c >= lag:
            down(c - lag)
    for c in range(max(n_chunks - lag, 0), n_chunks):
        finish_group()
        down(c)
    finish_group()
    y_ref[...] = DEEPNORM_ALPHA * x + FFN_RES * acc_ref[...]


def _ffn(x, w_in, w_out, g1, b1, pool=None, *, seq=None, tm=512, chunk=256, lag=2):
    ln_rows = 256 if pool is not None else 64
    n, d = x.shape
    d_ff = w_out.shape[0]
    n_tiles = n // tm
    assert n % tm == 0 and d_ff % chunk == 0 and w_in.shape == (d, 2 * d_ff)
    assert tm % ln_rows == 0 and tm // ln_rows <= d_ff // chunk and ln_rows >= POOL_HALO
    tile = _nbytes((tm, d), F32)
    row = pl.BlockSpec((1, d), _const2)
    in_specs = [
        pl.BlockSpec((tm, d), lambda i: (jnp.minimum(i, n_tiles - 1), 0)),
        pl.BlockSpec((d, 2 * d_ff), _const2, pipeline_mode=pl.Buffered(1)),
        pl.BlockSpec((d_ff, d), _const2, pipeline_mode=pl.Buffered(1)),
        row, row,
    ]
    args = [x, w_in, w_out, g1, b1]
    scratch = [pltpu.VMEM((tm, d), F32), pltpu.VMEM((tm, d), F32)]
    seq_tiles = None
    pool_bytes = 0
    if pool is not None:
        w_pool, scale, g2, b2 = pool
        assert seq % tm == 0 and max(POOL_WINDOWS) <= POOL_HALO
        assert all(w & (w - 1) == 0 for w in POOL_WINDOWS)
        seq_tiles = seq // tm
        in_specs += [pl.BlockSpec(w_pool.shape, lambda i: (0, 0, 0)), row, row, row]
        args += [w_pool, scale, g2, b2]
        scratch.append(pltpu.VMEM((POOL_HALO, d), F32))
        pool_bytes = 2 * _nbytes(w_pool.shape, BF16)
    vmem = _vmem_limit(_nbytes(w_in.shape, BF16), _nbytes(w_out.shape, BF16), pool_bytes,
                       4 * tile,
                       2 * tile,
                       3 * tile)
    return pl.pallas_call(
        functools.partial(_ffn_body, d_ff=d_ff, chunk=chunk, lag=lag, ln_rows=ln_rows,
                          seq_tiles=seq_tiles),
        out_shape=jax.ShapeDtypeStruct((n, d), F32),
        grid=(n_tiles + 1,),
        in_specs=in_specs,
        out_specs=pl.BlockSpec((tm, d), lambda i: (jnp.maximum(i - 1, 0), 0)),
        scratch_shapes=scratch,
        compiler_params=pltpu.CompilerParams(dimension_semantics=("arbitrary",),
                                             vmem_limit_bytes=vmem),
        name="ffn_ln" if pool is None else "ffn_ln_pool_ln",
    )(*args)


_NT_DIMS = (((1,), (1,)), ((), ()))


def _bf16_pieces(value, n):
    pieces = []
    for _ in range(n):
        piece = float(ml_dtypes.bfloat16(value))
        pieces.append(piece)
        value -= piece
    return pieces


LOG2E_PIECES = _bf16_pieces(math.log2(math.e), 3)
N_BIAS = 2 * len(LOG2E_PIECES)
V_ROWS_PAD = 16


def _qkv_body(x_ref, wqT_ref, wk_ref, wvT_ref, q1T_ref, q2T_ref, k1_ref, k2_ref, vT_ref,
              *, head_dim, scale):
    t = pl.program_id(1)
    ts = x_ref.shape[0]
    width = 2 * head_dim
    n_heads = vT_ref.shape[2] // (width + V_ROWS_PAD)
    xb = x_ref[...].astype(BF16)
    qT = lax.dot_general(wqT_ref[...], xb, _NT_DIMS, preferred_element_type=F32)
    qT = qT * (scale * math.log2(math.e))
    vT = lax.dot_general(wvT_ref[...], xb, _NT_DIMS, preferred_element_type=F32)
    kk = jnp.dot(xb, wk_ref[...], preferred_element_type=F32)

    rr = lax.broadcasted_iota(jnp.int32, (head_dim, ts), 0)
    pair = rr >> 1
    log2e_rows = jnp.where(rr < N_BIAS,
                           jnp.where(pair == 0, LOG2E_PIECES[0],
                                     jnp.where(pair == 1, LOG2E_PIECES[1], LOG2E_PIECES[2])),
                           0.0).astype(BF16)
    ones_rows = (lax.broadcasted_iota(jnp.int32, (V_ROWS_PAD, ts), 0) == 0).astype(BF16)

    lane = lax.broadcasted_iota(jnp.int32, (1, width), 1)
    pos = t * ts + lax.broadcasted_iota(jnp.int32, (ts, 1), 0)
    hi = ((pos >> POS_SPLIT_BITS) << POS_SPLIT_BITS).astype(F32)
    lo = (pos & ((1 << POS_SPLIT_BITS) - 1)).astype(F32)
    hi_lo = 0.5 * jnp.where((lane & 1) == 0, hi, lo)
    bias1 = jnp.where((lane >= head_dim) & (lane < head_dim + N_BIAS), hi_lo, 0.0)
    bias2 = jnp.where(lane < N_BIAS, hi_lo, 0.0)
    first_half = lane < head_dim

    for h in range(n_heads):
        lo_r, mid_r, hi_r = h * width, h * width + head_dim, (h + 1) * width
        q1T_ref[0, 0, lo_r:mid_r, :] = qT[lo_r:mid_r, :].astype(BF16)
        q1T_ref[0, 0, mid_r:hi_r, :] = log2e_rows
        q2T_ref[0, 0, lo_r:mid_r, :] = log2e_rows
        q2T_ref[0, 0, mid_r:hi_r, :] = qT[mid_r:hi_r, :].astype(BF16)

        base = h * (width + V_ROWS_PAD)
        vT_ref[0, 0, base:base + width, :] = vT[lo_r:hi_r, :].astype(BF16)
        vT_ref[0, 0, base + width:base + width + V_ROWS_PAD, :] = ones_rows

        k_h = kk[:, lo_r:hi_r]
        head_scale = 2.0 ** -h
        k1_ref[0, :, lo_r:hi_r] = jnp.where(first_half, k_h, bias1 * head_scale).astype(BF16)
        k2_ref[0, :, lo_r:hi_r] = jnp.where(first_half, bias2 * head_scale, k_h).astype(BF16)


def _qkv_proj(x, wqT, wk, wvT, *, batch, seq, ts, head_dim, scale):
    n, d = x.shape
    hw = wk.shape[1]
    width = 2 * head_dim
    n_heads = hw // width
    st = seq // ts
    assert seq % ts == 0 and seq <= MAX_SEQ and N_BIAS <= head_dim
    tile = _nbytes((ts, hw), F32)
    vmem = _vmem_limit(2 * 3 * _nbytes(wk.shape, BF16), 2 * _nbytes((ts, d), F32),
                       2 * 6 * _nbytes((ts, hw), BF16), 8 * tile)
    row_out = jax.ShapeDtypeStruct((batch, seq, hw), BF16)
    col_out = jax.ShapeDtypeStruct((batch, st, hw, ts), BF16)
    v_rows = n_heads * (width + V_ROWS_PAD)
    v_out = jax.ShapeDtypeStruct((batch, st, v_rows, ts), BF16)
    row_spec = pl.BlockSpec((1, ts, hw), lambda b, t: (b, t, 0))
    col_spec = pl.BlockSpec((1, 1, hw, ts), lambda b, t: (b, t, 0, 0))
    v_spec = pl.BlockSpec((1, 1, v_rows, ts), lambda b, t: (b, t, 0, 0))
    return pl.pallas_call(
        functools.partial(_qkv_body, head_dim=head_dim, scale=scale),
        out_shape=(col_out, col_out, row_out, row_out, v_out),
        grid=(batch, st),
        in_specs=[
            pl.BlockSpec((ts, d), lambda b, t: (b * st + t, 0)),
            pl.BlockSpec(wqT.shape, lambda b, t: (0, 0)),
            pl.BlockSpec(wk.shape, lambda b, t: (0, 0)),
            pl.BlockSpec(wvT.shape, lambda b, t: (0, 0)),
        ],
        out_specs=(col_spec, col_spec, row_spec, row_spec, v_spec),
        compiler_params=pltpu.CompilerParams(dimension_semantics=("arbitrary", "arbitrary"),
                                             vmem_limit_bytes=vmem),
        name="qkv_proj",
    )(x, wqT, wk, wvT)


def _attn_body(q1T_ref, q2T_ref, k1_ref, k2_ref, vT_ref, lam_ref, g_ref, o_ref,
               acc1, acc2, m1, m2, s_a, s_b, mx_a, mx_b, *, lambda_init):
    ts = q1T_ref.shape[-1]
    tq = 2 * ts
    n_tiles = q1T_ref.shape[1] // 2
    dv = o_ref.shape[-1]
    units = tuple((mp, hf) for mp in (0, 1) for hf in (0, 1))

    lam = (jnp.exp(jnp.sum(lam_ref[0:1, :] * lam_ref[1:2, :], axis=-1, keepdims=True))
           - jnp.exp(jnp.sum(lam_ref[2:3, :] * lam_ref[3:4, :], axis=-1, keepdims=True))
           + lambda_init)

    def scores(i, j, s_buf, mx_buf, mp, hf):
        rows = pl.ds(pl.multiple_of(j * ts, ts), ts)
        k = (k1_ref, k2_ref)[mp][0, rows, :]
        qT = (q1T_ref, q2T_ref)[mp][0, 2 * i + hf]
        s = jnp.dot(k, qT, preferred_element_type=F32)
        s_buf[mp, :, hf * ts:(hf + 1) * ts] = s
        mx_buf[mp, :, hf * ts:(hf + 1) * ts] = jnp.max(s, axis=0, keepdims=True)

    def softmax_pv(j, s_buf, mx_buf, mp, hf, causal=False):
        acc, m = ((acc1, m1), (acc2, m2))[mp]
        cols = slice(hf * ts, (hf + 1) * ts)
        v = vT_ref[0, j]
        s = s_buf[mp, :, cols]
        if causal:
            key_idx = lax.broadcasted_iota(jnp.int32, s.shape, 0)
            qry_idx = lax.broadcasted_iota(jnp.int32, s.shape, 1)
            s = jnp.where(key_idx <= qry_idx, s, -jnp.inf)
            blk_max = jnp.max(s, axis=0, keepdims=True)
        else:
            blk_max = mx_buf[mp, :, cols]
        m_old = m[:, cols]
        m_new = jnp.maximum(m_old, blk_max)
        alpha = jnp.exp2(m_old - m_new)
        p = jnp.exp2(s - m_new).astype(BF16)
        acc[:, cols] = alpha * acc[:, cols] + jnp.dot(v, p, preferred_element_type=F32)
        m[:, cols] = m_new

    def reset_stats():
        for m, acc in ((m1, acc1), (m2, acc2)):
            m[...] = jnp.full(m.shape, -jnp.inf, F32)
            acc[...] = jnp.zeros(acc.shape, F32)

    reset_stats()
    for mp, hf in units:
        scores(0, 0, s_a, mx_a, mp, hf)

    def query_tile(i, carry):
        def block_pair(j):
            for mp, hf in units:
                scores(i, j + 1, s_b, mx_b, mp, hf)
                softmax_pv(j, s_a, mx_a, mp, hf)
            for mp, hf in units:
                scores(i, j + 2, s_a, mx_a, mp, hf)
                softmax_pv(j + 1, s_b, mx_b, mp, hf)

        def four_pairs(t, inner):
            for u in range(4):
                block_pair(8 * t + 2 * u)
            return inner

        lax.fori_loop(0, lax.shift_right_logical(i, 2), four_pairs, 0)
        looped = (i >> 2) << 2

        def finish_tile():
            for mp in (0, 1):
                scores(i, 2 * i + 1, s_b, mx_b, mp, 1)
                softmax_pv(2 * i, s_a, mx_a, mp, 0, causal=True)
                softmax_pv(2 * i, s_a, mx_a, mp, 1)
            for mp in (0, 1):
                softmax_pv(2 * i + 1, s_b, mx_b, mp, 1, causal=True)

            o1T = acc1[0:dv, :] / acc1[dv:dv + 1, :]
            o2T = acc2[0:dv, :] / acc2[dv:dv + 1, :]
            o = (o1T - lam * o2T).T
            o = o * lax.rsqrt(jnp.mean(o * o, axis=-1, keepdims=True) + RMS_EPS)
            o = o * g_ref[...] * (1.0 - lambda_init)
            o_ref[0, pl.ds(pl.multiple_of(i * tq, tq), tq), :] = o.astype(o_ref.dtype)
            reset_stats()
            nxt = jnp.minimum(i + 1, n_tiles - 1)
            for mp, hf in units:
                scores(nxt, 0, s_a, mx_a, mp, hf)

        for rem in range(4):
            @pl.when((i & 3) == rem)
            def _(rem=rem):
                for u in range(rem):
                    block_pair(2 * (looped + u))
                finish_tile()

        return carry

    lax.fori_loop(0, n_tiles, query_tile, 0)


def _diff_attn(q1T, q2T, k1, k2, vT, lam_vecs, subln_g, *, lambda_init):
    batch, st, hw, ts = q1T.shape
    seq = st * ts
    width = hw // N_HEADS
    v_rows = width + V_ROWS_PAD
    tq = 2 * ts
    assert vT.shape == (batch, st, N_HEADS * v_rows, ts) and st % 2 == 0
    resident = 2 * (5 * _nbytes((seq, width), BF16) + _nbytes((st, v_rows, ts), BF16))
    score_buf = pltpu.VMEM((2, ts, tq), F32)
    max_buf = pltpu.VMEM((2, 1, tq), F32)
    acc_buf = pltpu.VMEM((v_rows, tq), F32)
    stat = pltpu.VMEM((1, tq), F32)
    vmem = _vmem_limit(resident,
                       2 * _nbytes((v_rows, tq), F32),
                       2 * _nbytes((2, ts, tq), F32),
                       4 * _nbytes((ts, tq), F32))
    q_spec = pl.BlockSpec((1, st, width, ts), lambda b, h: (b, 0, h, 0))
    k_spec = pl.BlockSpec((1, seq, width), lambda b, h: (b, 0, h))
    return pl.pallas_call(
        functools.partial(_attn_body, lambda_init=lambda_init),
        out_shape=jax.ShapeDtypeStruct((batch, seq, hw), BF16),
        grid=(batch, N_HEADS),
        in_specs=[
            q_spec, q_spec, k_spec, k_spec,
            pl.BlockSpec((1, st, v_rows, ts), lambda b, h: (b, 0, h, 0)),
            pl.BlockSpec(lam_vecs.shape, lambda b, h: (0, 0)),
            pl.BlockSpec((1, width), lambda b, h: (0, 0)),
        ],
        out_specs=pl.BlockSpec((1, seq, width), lambda b, h: (b, 0, h)),
        scratch_shapes=[acc_buf, acc_buf, stat, stat, score_buf, score_buf, max_buf, max_buf],
        compiler_params=pltpu.CompilerParams(
            dimension_semantics=("arbitrary", "arbitrary"), vmem_limit_bytes=vmem),
        name="diff_attn",
    )(q1T, q2T, k1, k2, vT, lam_vecs, subln_g)


def _proj_ln_body(x_ref, a_ref, w_ref, g_ref, b_ref, o_ref, *, sub):
    for r in range(x_ref.shape[0] // sub):
        rows = slice(r * sub, (r + 1) * sub)
        y = jnp.dot(a_ref[rows, :], w_ref[...], preferred_element_type=F32)
        o_ref[rows, :] = _layer_norm(DEEPNORM_ALPHA * x_ref[rows, :] + y, g_ref[...], b_ref[...])


def _proj_ln(x, a, w, g, b, *, tm=1024, sub=256):
    n, d = x.shape
    assert n % tm == 0 and tm % sub == 0
    tile = _nbytes((tm, d), F32)
    vmem = _vmem_limit(_nbytes(w.shape, BF16), 4 * tile, tile, 2 * tile)
    return pl.pallas_call(
        functools.partial(_proj_ln_body, sub=sub),
        out_shape=jax.ShapeDtypeStruct((n, d), F32),
        grid=(n // tm,),
        in_specs=[
            pl.BlockSpec((tm, d), lambda i: (i, 0)),
            pl.BlockSpec((tm, a.shape[1]), lambda i: (i, 0)),
            pl.BlockSpec(w.shape, _const2, pipeline_mode=pl.Buffered(1)),
            pl.BlockSpec((1, d), _const2),
            pl.BlockSpec((1, d), _const2),
        ],
        out_specs=pl.BlockSpec((tm, d), lambda i: (i, 0)),
        compiler_params=pltpu.CompilerParams(dimension_semantics=("arbitrary",),
                                             vmem_limit_bytes=vmem),
        name="proj_ln",
    )(x, a, w, g, b)


def _row(v):
    return v.reshape(1, -1).astype(F32)


def _cast_ffn(w_in, w_out, g, b):
    return w_in.astype(BF16), w_out.astype(BF16), _row(g), _row(b)


def kernel(x, l0_ffn1_w_in, l0_ffn1_w_out, l0_ln1_g, l0_ln1_b, l0_pool_w, l0_pool_scale, l0_ln2_g, l0_ln2_b, l0_ffn2_w_in, l0_ffn2_w_out, l0_ln3_g, l0_ln3_b, l1_ffn1_w_in, l1_ffn1_w_out, l1_ln1_g, l1_ln1_b, l1_w_qkv, l1_lam_q1, l1_lam_k1, l1_lam_q2, l1_lam_k2, l1_subln_g, l1_w_o, l1_ln2_g, l1_ln2_b, l1_ffn2_w_in, l1_ffn2_w_out, l1_ln3_g, l1_ln3_b):
    batch, seq, d = x.shape
    h = x.reshape(batch * seq, d)

    pool = (l0_pool_w.astype(BF16), _row(l0_pool_scale), _row(l0_ln2_g), _row(l0_ln2_b))
    h = _ffn(h, *_cast_ffn(l0_ffn1_w_in, l0_ffn1_w_out, l0_ln1_g, l0_ln1_b), pool, seq=seq)
    h = _ffn(h, *_cast_ffn(l0_ffn2_w_in, l0_ffn2_w_out, l0_ln3_g, l0_ln3_b))

    h = _ffn(h, *_cast_ffn(l1_ffn1_w_in, l1_ffn1_w_out, l1_ln1_g, l1_ln1_b))

    head_dim = d // (2 * N_HEADS)
    qk_w = N_HEADS * head_dim
    assert 2 * head_dim == V7X_LANES and l1_w_qkv.shape == (d, 4 * qk_w + N_HEADS * 2 * head_dim)

    def per_head_pairs(w_a, w_b):
        pair = jnp.stack([w_a.reshape(d, N_HEADS, head_dim), w_b.reshape(d, N_HEADS, head_dim)], 2)
        return pair.reshape(d, N_HEADS * 2 * head_dim)

    w = l1_w_qkv.astype(BF16)
    wqT = per_head_pairs(w[:, :qk_w], w[:, qk_w:2 * qk_w]).T
    wk = per_head_pairs(w[:, 2 * qk_w:3 * qk_w], w[:, 3 * qk_w:4 * qk_w])
    wvT = w[:, 4 * qk_w:].T
    q1T, q2T, k1, k2, vT = _qkv_proj(h, wqT, wk, wvT, batch=batch, seq=seq, ts=512,
                                     head_dim=head_dim, scale=head_dim ** -0.5)
    lambda_init = 0.8 - 0.6 * math.exp(-0.3 * 1)
    lam_vecs = jnp.stack([l1_lam_q1, l1_lam_k1, l1_lam_q2, l1_lam_k2]).astype(F32)
    attn = _diff_attn(q1T, q2T, k1, k2, vT, lam_vecs, _row(l1_subln_g), lambda_init=lambda_init)
    h = _proj_ln(h, attn.reshape(batch * seq, d), l1_w_o.astype(BF16), _row(l1_ln2_g),
                 _row(l1_ln2_b))

    h = _ffn(h, *_cast_ffn(l1_ffn2_w_in, l1_ffn2_w_out, l1_ln3_g, l1_ln3_b))
    return h.reshape(batch, seq, d)
```
